```python
import math
import jax, jax.numpy as jnp
from jax import lax
import numpy as np

D_MODEL = 1024
BATCH = 8
SEQ = 8192
DEPTH = 1
DEC_BATCH = 128
DEC_SEQ = 4
PAST_LEN = 8192
PAGE_SIZE = 128

N_HEADS = 16
HEAD_DIM = D_MODEL // N_HEADS
ATT_W = N_HEADS * HEAD_DIM
CONV_W = D_MODEL
CONV_K = 31
Q_BLOCK = 128
EPS = 1e-6
SB_SCALE = 1.0 / math.sqrt(HEAD_DIM)
SB_BIAS_INIT = -8.0
SPLITS = (ATT_W, ATT_W, ATT_W, ATT_W, CONV_W, CONV_W, CONV_W, D_MODEL, D_MODEL)
IN_COLS = 4 * ATT_W + 3 * CONV_W + 2 * D_MODEL

kernel_name = "stick_breaking_conformer_hybrid_step"


def rms_norm(x, g):
    xf = x.astype(jnp.float32)
    y = xf * lax.rsqrt(jnp.mean(xf * xf, axis=-1, keepdims=True) + EPS)
    return (y * g.astype(jnp.float32)).astype(x.dtype)


def layer_norm(x, g, b):
    xf = x.astype(jnp.float32)
    mu = jnp.mean(xf, axis=-1, keepdims=True)
    d = xf - mu
    y = d * lax.rsqrt(jnp.mean(d * d, axis=-1, keepdims=True) + EPS)
    return (y * g.astype(jnp.float32) + b.astype(jnp.float32)).astype(x.dtype)


def project(x, c, w_ada, b_ada, g_norm, w_in):
    mod = jax.nn.silu(c) @ w_ada + b_ada
    shift, scale, gate = jnp.split(mod[:, None, :], 3, axis=-1)
    h = rms_norm(x, g_norm) * (1 + scale) + shift
    p = h @ w_in
    parts = []
    off = 0
    for w in SPLITS:
        parts.append(p[..., off:off + w])
        off += w
    return gate, parts


def to_heads(t):
    return t.reshape(t.shape[0], t.shape[1], N_HEADS, HEAD_DIM)


def sb_step(acc, out, q, qpos, k, v, kpos, bias):
    z = jnp.einsum('nqhd,nkhd->nhqk', q, k, preferred_element_type=jnp.float32) * SB_SCALE
    z = z + bias.astype(jnp.float32)[None, :, None, None]
    valid = kpos[None, :] < qpos[:, None]
    log_1m_beta = jnp.where(valid, -jax.nn.softplus(z), 0.0)
    log_beta = jnp.where(valid, jax.nn.log_sigmoid(z), -jnp.inf)
    csum = jnp.cumsum(log_1m_beta, axis=-1)
    tot = csum[..., -1:]
    log_w = log_beta + (tot - csum) + acc[..., None]
    out = out + jnp.einsum('nhqk,nkhd->nhqd', jnp.exp(log_w), v.astype(jnp.float32))
    return acc + tot[..., 0], out


def sb_attention_prompt(q, k, v, bias):
    b, s = q.shape[0], q.shape[1]
    nb = s // Q_BLOCK
    blk = lambda t: t.reshape(b, nb, Q_BLOCK, N_HEADS, HEAD_DIM).transpose(1, 0, 2, 3, 4)
    qb, kb, vb = blk(q), blk(k), blk(v)
    ar = jnp.arange(Q_BLOCK, dtype=jnp.int32)

    def one_qblock(args):
        i, qi = args
        qpos = i * Q_BLOCK + ar

        def body(t, carry):
            j = i - t
            return sb_step(carry[0], carry[1], qi, qpos, kb[j], vb[j], j * Q_BLOCK + ar, bias)

        init = (jnp.zeros((b, N_HEADS, Q_BLOCK), jnp.float32),
                jnp.zeros((b, N_HEADS, Q_BLOCK, HEAD_DIM), jnp.float32))
        _, out = lax.fori_loop(0, i + 1, body, init)
        return out

    outs = lax.map(one_qblock, (jnp.arange(nb, dtype=jnp.int32), qb))
    return outs.transpose(1, 0, 3, 2, 4).reshape(b, s, ATT_W).astype(q.dtype)


def sb_attention_sample(q, k, v, cache_k, cache_v, page_table, bias):
    n, t = q.shape[0], q.shape[1]
    qpos = PAST_LEN + jnp.arange(t, dtype=jnp.int32)
    acc = jnp.zeros((n, N_HEADS, t), jnp.float32)
    out = jnp.zeros((n, N_HEADS, t, HEAD_DIM), jnp.float32)
    acc, out = sb_step(acc, out, q, qpos, k, v, qpos, bias)
    n_pages = PAST_LEN // PAGE_SIZE
    ar = jnp.arange(PAGE_SIZE, dtype=jnp.int32)

    def body(carry, xs):
        phys, p = xs
        carry = sb_step(carry[0], carry[1], q, qpos, cache_k[phys], cache_v[phys], p * PAGE_SIZE + ar, bias)
        return carry, None

    (acc, out), _ = lax.scan(body, (acc, out),
                             (page_table.T, jnp.arange(n_pages, dtype=jnp.int32)), reverse=True)
    return out.transpose(0, 2, 1, 3).reshape(n, t, ATT_W).astype(q.dtype)


def causal_dwconv(buf, w, b):
    out = lax.conv_general_dilated(buf, w[:, None, :].astype(buf.dtype), window_strides=(1,), padding='VALID',
                                   dimension_numbers=('NWC', 'WIO', 'NWC'),
                                   feature_group_count=buf.shape[-1])
    return out + b


def finish(x, gate, att, conv, ga, gc, ma, mc, ln_g, ln_b, w_a, w_c, w_out):
    conv = jax.nn.silu(layer_norm(conv, ln_g, ln_b))
    y_a = (att * jax.nn.silu(ga)) @ w_a
    y_c = (conv * jax.nn.silu(gc)) @ w_c
    merged = jax.nn.sigmoid(ma) * y_a + jax.nn.sigmoid(mc) * y_c
    return x + gate * (merged @ w_out)


def setup_inputs(seed: int = 0) -> dict:
    key = jax.random.key(seed)
    ks = jax.random.split(key, 24)
    n_pages = PAST_LEN // PAGE_SIZE
    n_used = DEC_BATCH * n_pages
    n_pool = (n_used * 5 + 3) // 4
    nrm = lambda k, shape, s: jax.random.normal(k, shape, jnp.float32) * s
    page_table = jax.random.permutation(ks[5], n_pool)[:n_used].reshape(DEC_BATCH, n_pages).astype(jnp.int32)
    return {
        "x_prompt": nrm(ks[0], (BATCH, SEQ, D_MODEL), 1.0),
        "x_sample": nrm(ks[1], (DEC_BATCH, DEC_SEQ, D_MODEL), 1.0),
        "cache_k": nrm(ks[2], (DEPTH, n_pool, PAGE_SIZE, N_HEADS, HEAD_DIM), 1.0),
        "cache_v": nrm(ks[3], (DEPTH, n_pool, PAGE_SIZE, N_HEADS, HEAD_DIM), 1.0),
        "state_conv": nrm(ks[4], (DEPTH, DEC_BATCH, CONV_K - 1, CONV_W), 0.5),
        "page_table": page_table,
        "c_prompt": nrm(ks[6], (BATCH, D_MODEL), 1.0),
        "c_sample": nrm(ks[7], (DEC_BATCH, D_MODEL), 1.0),
        "w_ada": nrm(ks[8], (DEPTH, D_MODEL, 3 * D_MODEL), 0.5 * D_MODEL ** -0.5),
        "b_ada": nrm(ks[9], (DEPTH, 3 * D_MODEL), 0.02),
        "g_norm": 1.0 + nrm(ks[10], (DEPTH, D_MODEL), 0.02),
        "w_in": nrm(ks[11], (DEPTH, D_MODEL, IN_COLS), D_MODEL ** -0.5),
        "b_sb": SB_BIAS_INIT + nrm(ks[20], (DEPTH, N_HEADS), 0.1),
        "w_dw": nrm(ks[12], (DEPTH, CONV_K, CONV_W), CONV_K ** -0.5),
        "b_dw": nrm(ks[13], (DEPTH, CONV_W), 0.02),
        "ln_g": 1.0 + nrm(ks[14], (DEPTH, CONV_W), 0.02),
        "ln_b": nrm(ks[15], (DEPTH, CONV_W), 0.02),
        "w_a": nrm(ks[16], (DEPTH, ATT_W, D_MODEL), ATT_W ** -0.5),
        "w_c": nrm(ks[17], (DEPTH, CONV_W, D_MODEL), CONV_W ** -0.5),
        "w_out": nrm(ks[18], (DEPTH, D_MODEL, D_MODEL), D_MODEL ** -0.5),
        "g_final": 1.0 + nrm(ks[19], (D_MODEL,), 0.02),
    }


def reference(x_prompt, x_sample, cache_k, cache_v, state_conv, page_table, c_prompt, c_sample,
              w_ada, b_ada, g_norm, w_in, b_sb, w_dw, b_dw, ln_g, ln_b, w_a, w_c, w_out, g_final):
    y_p, y_s = x_prompt, x_sample
    kp_l, vp_l, cp_l, ks_l, vs_l, cs_l = [], [], [], [], [], []
    for l in range(DEPTH):
        gate, (q, k, v, ga, ua, ub, gc, ma, mc) = project(y_p, c_prompt, w_ada[l], b_ada[l], g_norm[l], w_in[l])
        qh, kh, vh = to_heads(q), to_heads(k), to_heads(v)
        att = sb_attention_prompt(qh, kh, vh, b_sb[l])
        glu = ua * jax.nn.sigmoid(ub)
        conv = causal_dwconv(jnp.pad(glu, ((0, 0), (CONV_K - 1, 0), (0, 0))), w_dw[l], b_dw[l])
        y_p = finish(y_p, gate, att, conv, ga, gc, ma, mc, ln_g[l], ln_b[l], w_a[l], w_c[l], w_out[l])
        kp_l.append(kh)
        vp_l.append(vh)
        cp_l.append(glu[:, -(CONV_K - 1):])
        gate, (q, k, v, ga, ua, ub, gc, ma, mc) = project(y_s, c_sample, w_ada[l], b_ada[l], g_norm[l], w_in[l])
        qh, kh, vh = to_heads(q), to_heads(k), to_heads(v)
        att = sb_attention_sample(qh, kh, vh, cache_k[l], cache_v[l], page_table, b_sb[l])
        glu = ua * jax.nn.sigmoid(ub)
        buf = jnp.concatenate([state_conv[l].astype(glu.dtype), glu], axis=1)
        conv = causal_dwconv(buf, w_dw[l], b_dw[l])
        y_s = finish(y_s, gate, att, conv, ga, gc, ma, mc, ln_g[l], ln_b[l], w_a[l], w_c[l], w_out[l])
        ks_l.append(kh)
        vs_l.append(vh)
        cs_l.append(buf[:, -(CONV_K - 1):])
    y_prompt = rms_norm(y_p, g_final)
    y_sample = rms_norm(y_s, g_final)
    k_prompt = jnp.stack(kp_l)
    v_prompt = jnp.stack(vp_l)
    conv_prompt = jnp.stack(cp_l)
    k_sample = jnp.stack(ks_l)
    v_sample = jnp.stack(vs_l)
    conv_sample = jnp.stack(cs_l)
    return (y_prompt, y_sample, k_prompt, v_prompt, conv_prompt, k_sample, v_sample, conv_sample)
```

```python
import functools

import jax
import jax.numpy as jnp
from jax import lax
from jax.experimental import pallas as pl
from jax.experimental.pallas import tpu as pltpu

F32 = jnp.float32
BF16 = jnp.bfloat16

D_MODEL = 1024
N_HEADS = 16
HEAD_DIM = 64
CONV_K = 31
PAGE = 128
EPS = 1e-6
SB_SCALE = 0.125

LANES = 128
SUBLANES = 8
VMEM_LIMIT = 56 * 1024 * 1024

PROJ_TM = 512
ATT_T = 256
FIN_TM = 256
HALO = 32
SAMPLE_PAGES = 8

NT_DIMS = (((1,), (1,)), ((), ()))


def _sigmoid(x):
    return 1.0 / (1.0 + jnp.exp(-x))


def _silu(x):
    return x * _sigmoid(x)


def _neg_softplus(z):
    nz = -z
    t = jnp.exp(jnp.minimum(z, nz))
    return jnp.minimum(nz, 0.0) - jnp.log(1.0 + t)


def _strict_upper(n):
    r = lax.broadcasted_iota(jnp.int32, (n, n), 0)
    c = lax.broadcasted_iota(jnp.int32, (n, n), 1)
    return jnp.where(r > c, 1.0, 0.0).astype(BF16)


def _mod_kernel(c_ref, w_ref, b_ref, o_ref):
    s = _silu(c_ref[...])
    o_ref[...] = jnp.dot(s.astype(BF16), w_ref[...], preferred_element_type=F32) + b_ref[...]


def _mod(c, w_bf, b):
    n = c.shape[0]
    return pl.pallas_call(
        _mod_kernel,
        out_shape=jax.ShapeDtypeStruct((n, 3 * D_MODEL), F32),
        compiler_params=pltpu.CompilerParams(vmem_limit_bytes=VMEM_LIMIT),
        name="mod",
    )(c, w_bf, b)


def _proj_kernel(x_ref, shift_ref, scale_ref, g_ref, w_ref,
                 q_ref, kt_ref, ktb_ref, vt_ref, vtb_ref, sga_ref, glu_ref, sgc_ref, sma_ref, smc_ref,
                 h_scr):
    j = pl.program_id(1)
    nkb = ktb_ref.shape[1]
    half = D_MODEL // 2

    @pl.when(j == 0)
    def _():
        x = x_ref[...]
        y = x * lax.rsqrt(jnp.mean(x * x, axis=-1, keepdims=True) + EPS) * g_ref[...]
        h_scr[...] = (y * (1.0 + scale_ref[0]) + shift_ref[0]).astype(BF16)

    def nn():
        return jnp.dot(h_scr[...], w_ref[0], preferred_element_type=F32)

    def nt():
        return lax.dot_general(w_ref[0], h_scr[...], NT_DIMS, preferred_element_type=F32)

    @pl.when(j == 0)
    def _():
        q_ref[...] = (nn() * SB_SCALE).astype(BF16)

    def store_t(full_ref, blk_ref):
        r = nt()
        full_ref[0] = r
        rb = r.astype(BF16)
        for b in range(nkb):
            blk_ref[0, b] = rb[:, b * ATT_T:(b + 1) * ATT_T]

    @pl.when(j == 1)
    def _():
        store_t(kt_ref, ktb_ref)

    @pl.when(j == 2)
    def _():
        store_t(vt_ref, vtb_ref)

    @pl.when(j == 3)
    def _():
        sga_ref[...] = _silu(nn()).astype(BF16)

    @pl.when(j == 4)
    def _():
        r = nn()
        glu_ref[:, :half] = r[:, :half] * _sigmoid(r[:, half:])

    @pl.when(j == 5)
    def _():
        r = nn()
        glu_ref[:, half:] = r[:, :half] * _sigmoid(r[:, half:])

    @pl.when(j == 6)
    def _():
        sgc_ref[...] = _silu(nn()).astype(BF16)

    @pl.when(j == 7)
    def _():
        sma_ref[...] = _sigmoid(nn()).astype(BF16)

    @pl.when(j == 8)
    def _():
        smc_ref[...] = _sigmoid(nn()).astype(BF16)


def _proj(x, shift, scale, g, w9, n_seq, seq_len, per_row_mod):
    t = x.shape[0]
    tm = min(PROJ_TM, t)
    nblk = t // tm
    bps = seq_len // tm
    nkb = tm // ATT_T
    tok = lambda i, j: (i, 0)
    if per_row_mod:
        mod_spec = pl.BlockSpec((1, tm, D_MODEL), lambda i, j: (0, i, 0))
    else:
        mod_spec = pl.BlockSpec((1, 1, D_MODEL), lambda i, j: (i // bps, 0, 0))
    tspec = pl.BlockSpec((1, D_MODEL, tm), lambda i, j: (i // bps, 0, i % bps))
    tbspec = pl.BlockSpec((1, nkb, D_MODEL, ATT_T), lambda i, j: (i // bps, i % bps, 0, 0))
    row_bf = jax.ShapeDtypeStruct((t, D_MODEL), BF16)
    t_f32 = jax.ShapeDtypeStruct((n_seq, D_MODEL, seq_len), F32)
    t_bf = jax.ShapeDtypeStruct((n_seq, seq_len // ATT_T, D_MODEL, ATT_T), BF16)
    return pl.pallas_call(
        _proj_kernel,
        grid=(nblk, 9),
        in_specs=[
            pl.BlockSpec((tm, D_MODEL), tok),
            mod_spec, mod_spec,
            pl.BlockSpec((1, D_MODEL), lambda i, j: (0, 0)),
            pl.BlockSpec((1, D_MODEL, D_MODEL), lambda i, j: (j, 0, 0)),
        ],
        out_specs=[
            pl.BlockSpec((tm, D_MODEL), tok),
            tspec, tbspec,
            tspec, tbspec,
            pl.BlockSpec((tm, D_MODEL), tok),
            pl.BlockSpec((tm, D_MODEL), tok),
            pl.BlockSpec((tm, D_MODEL), tok),
            pl.BlockSpec((tm, D_MODEL), tok),
            pl.BlockSpec((tm, D_MODEL), tok),
        ],
        out_shape=[row_bf, t_f32, t_bf, t_f32, t_bf, row_bf,
                   jax.ShapeDtypeStruct((t, D_MODEL), F32), row_bf, row_bf, row_bf],
        scratch_shapes=[pltpu.VMEM((tm, D_MODEL), BF16)],
        compiler_params=pltpu.CompilerParams(
            dimension_semantics=("arbitrary", "arbitrary"), vmem_limit_bytes=VMEM_LIMIT),
        name="proj",
    )(x, shift, scale, g, w9)


def _attn_prompt_kernel(bias_ref, q_ref, kt_ref, vt_ref, o_ref, oacc, acc):
    hp = pl.program_id(1)
    i = pl.program_id(2)
    t = ATT_T
    q2 = q_ref[...].astype(F32)
    lane = lax.broadcasted_iota(jnp.int32, (t, LANES), 1)
    qm = jnp.concatenate([jnp.where(lane < HEAD_DIM, q2, 0.0),
                          jnp.where(lane >= HEAD_DIM, q2, 0.0)], axis=0).astype(BF16)
    b0 = bias_ref[2 * hp]
    b1 = bias_ref[2 * hp + 1]
    upper = _strict_upper(t)
    oacc[...] = jnp.zeros_like(oacc)
    acc[...] = jnp.zeros_like(acc)

    def step(j, masked):
        s = jnp.dot(qm, kt_ref[0, j], preferred_element_type=F32)
        z = jnp.concatenate([s[:t] + b0, s[t:] + b1], axis=0)
        l1m = _neg_softplus(z)
        if masked:
            qpos = lax.broadcasted_iota(jnp.int32, (t, t), 0)
            kpos = lax.broadcasted_iota(jnp.int32, (t, t), 1)
            valid = kpos < qpos
            valid = jnp.concatenate([valid, valid], axis=0)
            l1m = jnp.where(valid, l1m, 0.0)
        suffix = jnp.dot(l1m.astype(BF16), upper, preferred_element_type=F32)
        a = acc[...]
        logw = (z + l1m) + suffix + jnp.concatenate([a] * (t // LANES), axis=1)
        w = jnp.exp(logw)
        if masked:
            w = jnp.where(valid, w, 0.0)
        oacc[...] += lax.dot_general(w.astype(BF16), vt_ref[0, j], NT_DIMS, preferred_element_type=F32)
        acc[...] = a + jnp.sum(l1m, axis=-1, keepdims=True)

    step(i, True)

    def body(n, carry):
        step(i - 1 - n, False)
        return carry

    lax.fori_loop(0, i, body, 0)
    o = oacc[...]
    o_ref[...] = jnp.where(lane < HEAD_DIM, o[:t], o[t:]).astype(BF16)


def _attn_prompt(bias, qb, ktb, vtb, n_seq, seq_len):
    t = ATT_T
    nq = seq_len // t
    npair = N_HEADS // 2
    kv_spec = pl.BlockSpec((1, nq, LANES, t), lambda b, hp, i, *_: (b, 0, hp, 0))
    return pl.pallas_call(
        _attn_prompt_kernel,
        grid_spec=pltpu.PrefetchScalarGridSpec(
            num_scalar_prefetch=1,
            grid=(n_seq, npair, nq),
            in_specs=[pl.BlockSpec((t, LANES), lambda b, hp, i, *_: (b * nq + i, hp)), kv_spec, kv_spec],
            out_specs=pl.BlockSpec((t, LANES), lambda b, hp, i, *_: (b * nq + i, hp)),
            scratch_shapes=[pltpu.VMEM((2 * t, LANES), F32), pltpu.VMEM((2 * t, LANES), F32)],
        ),
        out_shape=jax.ShapeDtypeStruct((n_seq * seq_len, D_MODEL), BF16),
        compiler_params=pltpu.CompilerParams(
            dimension_semantics=("arbitrary", "arbitrary", "arbitrary"), vmem_limit_bytes=VMEM_LIMIT),
        name="attn_prompt",
    )(bias, qb, ktb, vtb)


def _attn_sample_kernel(pt_ref, q_ref, kn_ref, vn_ref, bias_ref, *refs):
    del pt_ref
    p = SAMPLE_PAGES
    k_refs = refs[:p]
    v_refs = refs[p:2 * p]
    o_ref = refs[2 * p]
    qbd, oacc, acc = refs[2 * p + 1:]
    c = pl.program_id(1)
    nrow = q_ref.shape[1] * N_HEADS
    n_new = q_ref.shape[1]
    row = lax.broadcasted_iota(jnp.int32, (N_HEADS, D_MODEL), 0)
    col_head = lax.broadcasted_iota(jnp.int32, (N_HEADS, D_MODEL), 1) // HEAD_DIM
    head_mask = row == col_head
    upper = _strict_upper(PAGE)
    bias = bias_ref[...]

    def fold(z, w_to_out, valid=None):
        z = z + bias
        l1m = _neg_softplus(z)
        if valid is not None:
            l1m = jnp.where(valid, l1m, 0.0)
        suffix = jnp.dot(l1m.astype(BF16), upper, preferred_element_type=F32)
        a = acc[...]
        w = jnp.exp((z + l1m) + suffix + a)
        if valid is not None:
            w = jnp.where(valid, w, 0.0)
        oacc[...] += w_to_out(w.astype(BF16))
        acc[...] = a + jnp.sum(l1m, axis=-1, keepdims=True)

    @pl.when(c == 0)
    def _():
        q4 = q_ref[0]
        qbd[...] = jnp.concatenate(
            [jnp.where(head_mask, jnp.broadcast_to(q4[t:t + 1, :], (N_HEADS, D_MODEL)), 0.0)
             for t in range(n_new)], axis=0).astype(BF16)
        oacc[...] = jnp.zeros_like(oacc)
        acc[...] = jnp.zeros_like(acc)
        pad = jnp.zeros((PAGE - kn_ref.shape[1], D_MODEL), F32)
        kn = jnp.concatenate([kn_ref[0], pad], axis=0).astype(BF16)
        vn = jnp.concatenate([vn_ref[0], pad], axis=0).astype(BF16)
        z = lax.dot_general(qbd[...], kn, NT_DIMS, preferred_element_type=F32)
        qtok = lax.broadcasted_iota(jnp.int32, (nrow, PAGE), 0) // N_HEADS
        ktok = lax.broadcasted_iota(jnp.int32, (nrow, PAGE), 1)
        fold(z, lambda w: jnp.dot(w, vn, preferred_element_type=F32), valid=ktok < qtok)

    for r in reversed(range(p)):
        kt = k_refs[r][0].astype(BF16)
        vt = v_refs[r][0].astype(BF16)
        z = jnp.dot(qbd[...], kt, preferred_element_type=F32)
        fold(z, lambda w: lax.dot_general(w, vt, NT_DIMS, preferred_element_type=F32))

    @pl.when(c == pl.num_programs(1) - 1)
    def _():
        o = oacc[...]
        for t in range(n_new):
            blk = jnp.where(head_mask, o[t * N_HEADS:(t + 1) * N_HEADS], 0.0)
            o_ref[0, pl.ds(t, 1), :] = jnp.sum(blk, axis=0, keepdims=True)


def _attn_sample(pt_t, q, kn, vn, bias_rows, kt_pages, vt_pages):
    n_pages, n_seq = pt_t.shape
    n_new = q.shape[1]
    p = SAMPLE_PAGES
    nch = n_pages // p
    nrow = n_new * N_HEADS

    def page_spec(r):
        return pl.BlockSpec((1, D_MODEL, PAGE), lambda n, c, pt: (pt[(nch - 1 - c) * p + r, n], 0, 0))

    per_seq = lambda rows: pl.BlockSpec((1, rows, D_MODEL), lambda n, c, pt: (n, 0, 0))
    return pl.pallas_call(
        _attn_sample_kernel,
        grid_spec=pltpu.PrefetchScalarGridSpec(
            num_scalar_prefetch=1,
            grid=(n_seq, nch),
            in_specs=[per_seq(n_new), per_seq(kn.shape[1]), per_seq(vn.shape[1]),
                      pl.BlockSpec((nrow, LANES), lambda n, c, pt: (0, 0))]
                     + [page_spec(r) for r in range(p)] + [page_spec(r) for r in range(p)],
            out_specs=per_seq(n_new),
            scratch_shapes=[pltpu.VMEM((nrow, D_MODEL), BF16), pltpu.VMEM((nrow, D_MODEL), F32),
                            pltpu.VMEM((nrow, LANES), F32)],
        ),
        out_shape=jax.ShapeDtypeStruct((n_seq, n_new, D_MODEL), F32),
        compiler_params=pltpu.CompilerParams(
            dimension_semantics=("arbitrary", "arbitrary"), vmem_limit_bytes=VMEM_LIMIT),
        name="attn_sample",
    )(pt_t, q, kn, vn, bias_rows, *([kt_pages] * p), *([vt_pages] * p))


def _conv_sample_kernel(state_ref, glu_ref, w_ref, b_ref, o_ref):
    n_state = state_ref.shape[0]
    n_new = glu_ref.shape[0]
    for t in range(n_new):
        a = jnp.broadcast_to(b_ref[...], o_ref.shape[1:])
        for j in range(CONV_K):
            i = t + j
            src = state_ref[i] if i < n_state else glu_ref[i - n_state]
            a = a + w_ref[pl.ds(j, 1), :] * src
        o_ref[t] = a


def _conv_sample(state, glu, w_dw, b_dw):
    n_state, n, ch = state.shape
    n_new = glu.shape[0]
    cw = LANES
    return pl.pallas_call(
        _conv_sample_kernel,
        grid=(ch // cw,),
        in_specs=[pl.BlockSpec((n_state, n, cw), lambda c: (0, 0, c)),
                  pl.BlockSpec((n_new, n, cw), lambda c: (0, 0, c)),
                  pl.BlockSpec((CONV_K, cw), lambda c: (0, c)),
                  pl.BlockSpec((1, cw), lambda c: (0, c))],
        out_specs=pl.BlockSpec((n_new, n, cw), lambda c: (0, 0, c)),
        out_shape=jax.ShapeDtypeStruct((n_new, n, ch), F32),
        compiler_params=pltpu.CompilerParams(dimension_semantics=("arbitrary",), vmem_limit_bytes=VMEM_LIMIT),
        name="conv_sample",
    )(state, glu, w_dw, b_dw)


def _finish_core(x, gate, att, sga, conv, sgc, sma, smc, lng, lnb, wa, wc, wo, gf):
    mu = jnp.mean(conv, axis=-1, keepdims=True)
    d = conv - mu
    ln = d * lax.rsqrt(jnp.mean(d * d, axis=-1, keepdims=True) + EPS) * lng + lnb
    cv = (_silu(ln) * sgc.astype(F32)).astype(BF16)
    av = (att.astype(F32) * sga.astype(F32)).astype(BF16)
    y_a = jnp.dot(av, wa, preferred_element_type=F32)
    y_c = jnp.dot(cv, wc, preferred_element_type=F32)
    merged = (sma.astype(F32) * y_a + smc.astype(F32) * y_c).astype(BF16)
    y = x + gate * jnp.dot(merged, wo, preferred_element_type=F32)
    return y * lax.rsqrt(jnp.mean(y * y, axis=-1, keepdims=True) + EPS) * gf


def _finish_prompt_kernel(x_ref, gate_ref, att_ref, sga_ref, glu_ref, halo_ref, sgc_ref, sma_ref, smc_ref,
                          wdw_ref, bdw_ref, lng_ref, lnb_ref, wa_ref, wc_ref, wo_ref, gf_ref, o_ref, buf,
                          *, blocks_per_seq):
    tm = x_ref.shape[0]
    first = (pl.program_id(0) % blocks_per_seq) == 0
    halo = halo_ref[...]
    buf[:HALO] = jnp.where(first, jnp.zeros_like(halo), halo)
    buf[HALO:] = glu_ref[...]
    base = HALO - (CONV_K - 1)
    conv = jnp.broadcast_to(bdw_ref[...], (tm, D_MODEL))
    for sub in range(SUBLANES):
        rows = tm + HALO - (SUBLANES if sub else 0)
        shifted = buf[pl.ds(sub, rows), :]
        for a in range((rows - tm) // SUBLANES + 1):
            j = SUBLANES * a + sub - base
            if 0 <= j < CONV_K:
                conv = conv + wdw_ref[pl.ds(j, 1), :] * shifted[SUBLANES * a:SUBLANES * a + tm]
    o_ref[...] = _finish_core(x_ref[...], gate_ref[0], att_ref[...], sga_ref[...], conv, sgc_ref[...],
                              sma_ref[...], smc_ref[...], lng_ref[...], lnb_ref[...], wa_ref[...],
                              wc_ref[...], wo_ref[...], gf_ref[...])


def _finish_sample_kernel(x_ref, gate_ref, att_ref, sga_ref, conv_ref, sgc_ref, sma_ref, smc_ref,
                          lng_ref, lnb_ref, wa_ref, wc_ref, wo_ref, gf_ref, o_ref):
    o_ref[...] = _finish_core(x_ref[...], gate_ref[0], att_ref[...], sga_ref[...], conv_ref[...], sgc_ref[...],
                              sma_ref[...], smc_ref[...], lng_ref[...], lnb_ref[...], wa_ref[...],
                              wc_ref[...], wo_ref[...], gf_ref[...])


def _const_spec(shape):
    return pl.BlockSpec(shape, lambda i: (0,) * len(shape))


def _finish_prompt(x, gate, att, sga, glu, sgc, sma, smc, wdw, bdw, lng, lnb, wa, wc, wo, gf, seq_len):
    t = x.shape[0]
    tm = FIN_TM
    bps = seq_len // tm
    tok = pl.BlockSpec((tm, D_MODEL), lambda i: (i, 0))
    vec = _const_spec((1, D_MODEL))
    mat = _const_spec((D_MODEL, D_MODEL))
    halo_spec = pl.BlockSpec((HALO, D_MODEL), lambda i: (jnp.maximum(i * (tm // HALO) - 1, 0), 0))
    return pl.pallas_call(
        functools.partial(_finish_prompt_kernel, blocks_per_seq=bps),
        grid=(t // tm,),
        in_specs=[tok, pl.BlockSpec((1, 1, D_MODEL), lambda i: (i // bps, 0, 0)), tok, tok, tok, halo_spec,
                  tok, tok, tok, _const_spec((CONV_K, D_MODEL)), vec, vec, vec, mat, mat, mat, vec],
        out_specs=tok,
        out_shape=jax.ShapeDtypeStruct((t, D_MODEL), F32),
        scratch_shapes=[pltpu.VMEM((tm + HALO, D_MODEL), F32)],
        compiler_params=pltpu.CompilerParams(dimension_semantics=("arbitrary",), vmem_limit_bytes=VMEM_LIMIT),
        name="finish_prompt",
    )(x, gate, att, sga, glu, glu, sgc, sma, smc, wdw, bdw, lng, lnb, wa, wc, wo, gf)


def _finish_sample(x, gate, att, sga, conv, sgc, sma, smc, lng, lnb, wa, wc, wo, gf):
    t = x.shape[0]
    tok = _const_spec((t, D_MODEL))
    vec = _const_spec((1, D_MODEL))
    mat = _const_spec((D_MODEL, D_MODEL))
    return pl.pallas_call(
        _finish_sample_kernel,
        grid=(1,),
        in_specs=[tok, _const_spec((1, t, D_MODEL)), tok, tok, tok, tok, tok, tok, vec, vec, mat, mat, mat, vec],
        out_specs=tok,
        out_shape=jax.ShapeDtypeStruct((t, D_MODEL), F32),
        compiler_params=pltpu.CompilerParams(dimension_semantics=("arbitrary",), vmem_limit_bytes=VMEM_LIMIT),
        name="finish_sample",
    )(x, gate, att, sga, conv, sgc, sma, smc, lng, lnb, wa, wc, wo, gf)


def kernel(x_prompt, x_sample, cache_k, cache_v, state_conv, page_table, c_prompt, c_sample, w_ada, b_ada, g_norm, w_in, b_sb, w_dw, b_dw, ln_g, ln_b, w_a, w_c, w_out, g_final):
    assert w_in.shape[0] == 1, "one layer"
    n_p, s_p, d = x_prompt.shape
    n_s, s_s, _ = x_sample.shape
    n_state = state_conv.shape[2]
    half = d // 2

    w = w_in[0]
    grp = [w[:, g * d:(g + 1) * d] for g in range(9)]
    w9 = jnp.stack([
        grp[0], grp[1].T, grp[2].T, grp[3],
        jnp.concatenate([grp[4][:, :half], grp[5][:, :half]], axis=1),
        jnp.concatenate([grp[4][:, half:], grp[5][:, half:]], axis=1),
        grp[6], grp[7], grp[8]]).astype(BF16)
    wa, wc, wo = w_a[0].astype(BF16), w_c[0].astype(BF16), w_out[0].astype(BF16)
    g1, gf = g_norm[0][None, :], g_final[None, :]
    lng, lnb, bdw, wdw = ln_g[0][None, :], ln_b[0][None, :], b_dw[0][None, :], w_dw[0]
    bias = b_sb[0]

    mod = _mod(jnp.concatenate([c_prompt, c_sample], axis=0), w_ada[0].astype(BF16), b_ada[0][None, :])
    mod_p = mod[:n_p].reshape(n_p, 3, d)
    mod_s = mod[n_p:].reshape(n_s, 3, d)
    mod_s_rows = jnp.broadcast_to(mod_s[None], (s_s, n_s, 3, d)).reshape(s_s * n_s, 3, d)

    xp = x_prompt.reshape(n_p * s_p, d)
    shift_p, scale_p, gate_p = (mod_p[:, c][:, None, :] for c in range(3))
    qb, kt, ktb, vt, vtb, sga, glu, sgc, sma, smc = _proj(xp, shift_p, scale_p, g1, w9, n_p, s_p, False)
    att = _attn_prompt(bias, qb, ktb, vtb, n_p, s_p)
    y_p = _finish_prompt(xp, gate_p, att, sga, glu, sgc, sma, smc, wdw, bdw, lng, lnb, wa, wc, wo, gf, s_p)
    y_prompt = y_p.reshape(n_p, s_p, d)
    to_heads_t = lambda a: a.reshape(a.shape[0], N_HEADS, HEAD_DIM, a.shape[2]).transpose(0, 3, 1, 2)[None]
    k_prompt, v_prompt = to_heads_t(kt), to_heads_t(vt)
    conv_prompt = glu.reshape(n_p, s_p, d)[:, s_p - (CONV_K - 1):][None]

    xs = x_sample.transpose(1, 0, 2).reshape(s_s * n_s, d)
    shift_s, scale_s, gate_s = (mod_s_rows[:, c][None] for c in range(3))
    qb_s, kt_s, _, vt_s, _, sga_s, glu_s, sgc_s, sma_s, smc_s = _proj(
        xs, shift_s, scale_s, g1, w9, 1, s_s * n_s, True)
    tok_major = lambda a_t: a_t[0].T.reshape(s_s, n_s, d)
    k_new, v_new = tok_major(kt_s), tok_major(vt_s)
    seq_major = lambda a: a.transpose(1, 0, 2)
    pad_rows = lambda a: jnp.pad(a, ((0, 0), (0, SUBLANES - s_s), (0, 0)))
    q_seq = seq_major(qb_s.astype(F32).reshape(s_s, n_s, d))
    pages_t = lambda c: c[0].transpose(0, 2, 3, 1).reshape(c.shape[1], d, PAGE)
    bias_rows = jnp.broadcast_to(jnp.tile(bias, s_s)[:, None], (s_s * N_HEADS, LANES))
    att_s = _attn_sample(page_table.T, q_seq, pad_rows(seq_major(k_new)), pad_rows(seq_major(v_new)),
                         bias_rows, pages_t(cache_k), pages_t(cache_v))
    att_s = seq_major(att_s).reshape(s_s * n_s, d).astype(BF16)
    state_t = state_conv[0].transpose(1, 0, 2)
    glu_s3 = glu_s.reshape(s_s, n_s, d)
    conv_s = _conv_sample(state_t, glu_s3, wdw, bdw).reshape(s_s * n_s, d)
    y_s = _finish_sample(xs, gate_s, att_s, sga_s, conv_s, sgc_s, sma_s, smc_s, lng, lnb, wa, wc, wo, gf)
    y_sample = seq_major(y_s.reshape(s_s, n_s, d))
    heads = lambda a: seq_major(a).reshape(n_s, s_s, N_HEADS, HEAD_DIM)[None]
    k_sample, v_sample = heads(k_new), heads(v_new)
    conv_sample = seq_major(jnp.concatenate([state_t, glu_s3], axis=0)[s_s:])[None]
    assert n_state == CONV_K - 1
    return (y_prompt, y_sample, k_prompt, v_prompt, conv_prompt, k_sample, v_sample, conv_sample)
```

```python
import functools

import jax
import jax.numpy as jnp
from jax import lax
from jax.experimental import pallas as pl
from jax.experimental.pallas import tpu as pltpu

F32 = jnp.float32
BF16 = jnp.bfloat16

D_MODEL = 1024
N_HEADS = 16
HEAD_DIM = 64
CONV_K = 31
PAGE = 128
EPS = 1e-6
SB_SCALE = 0.125
LOG2E = 1.4426950408889634

LANES = 128
SUBLANES = 8
VMEM_LIMIT = 56 * 1024 * 1024

PROJ_TM = 256
ATT_T = 256
ATT_GROUPS = 4
FIN_TM = 256
HALO = 32
SAMPLE_PAGES = 8
SAMPLE_BLOCK = 256

NT_DIMS = (((1,), (1,)), ((), ()))


def _sigmoid(x):
    return 1.0 / (1.0 + jnp.exp(-x))


def _silu(x):
    return x * _sigmoid(x)


def _softplus(z):
    return jnp.maximum(z, 0.0) + jnp.log(1.0 + jnp.exp2(jnp.abs(z) * (-LOG2E)))


def _neg_softplus(z):
    return -_softplus(z)


def _strict_upper(n):
    r = lax.broadcasted_iota(jnp.int32, (n, n), 0)
    c = lax.broadcasted_iota(jnp.int32, (n, n), 1)
    return jnp.where(r > c, 1.0, 0.0).astype(BF16)


def _neg_upper_incl(n):
    r = lax.broadcasted_iota(jnp.int32, (n, n), 0)
    c = lax.broadcasted_iota(jnp.int32, (n, n), 1)
    return jnp.where(r >= c, -1.0, 0.0).astype(BF16)


def _mod_kernel(c_ref, w_ref, b_ref, o_ref):
    s = _silu(c_ref[...])
    o_ref[...] = jnp.dot(s.astype(BF16), w_ref[...], preferred_element_type=F32) + b_ref[...]


def _mod(c, w_bf, b):
    n = c.shape[0]
    return pl.pallas_call(
        _mod_kernel,
        out_shape=jax.ShapeDtypeStruct((n, 3 * D_MODEL), F32),
        compiler_params=pltpu.CompilerParams(vmem_limit_bytes=VMEM_LIMIT),
        name="mod",
    )(c, w_bf, b)


def _proj_kernel(x_ref, shift_ref, scale_ref, g_ref, w_ref,
                 q_ref, kt_ref, ktb_ref, vt_ref, vtb_ref, sga_ref, glu_ref, sgc_ref, sma_ref, smc_ref,
                 h_scr):
    nkb = ktb_ref.shape[1]
    half = D_MODEL // 2
    x = x_ref[...]
    y = x * lax.rsqrt(jnp.mean(x * x, axis=-1, keepdims=True) + EPS) * g_ref[...]
    h_scr[...] = (y * (1.0 + scale_ref[0]) + shift_ref[0]).astype(BF16)

    def nn(j):
        return jnp.dot(h_scr[...], w_ref[j], preferred_element_type=F32)

    def nt(j):
        return lax.dot_general(w_ref[j], h_scr[...], NT_DIMS, preferred_element_type=F32)

    def store_t(j, full_ref, blk_ref):
        r = nt(j)
        full_ref[0] = r
        rb = r.astype(BF16)
        for b in range(nkb):
            blk_ref[0, b] = rb[:, b * ATT_T:(b + 1) * ATT_T]

    q_ref[...] = (nn(0) * SB_SCALE).astype(BF16)
    store_t(1, kt_ref, ktb_ref)
    store_t(2, vt_ref, vtb_ref)
    sga_ref[...] = _silu(nn(3)).astype(BF16)
    for part in range(2):
        r = nn(4 + part)
        glu_ref[:, part * half:(part + 1) * half] = r[:, :half] * _sigmoid(r[:, half:])
    sgc_ref[...] = _silu(nn(6)).astype(BF16)
    sma_ref[...] = _sigmoid(nn(7)).astype(BF16)
    smc_ref[...] = _sigmoid(nn(8)).astype(BF16)


def _proj(x, shift, scale, g, w9, n_seq, seq_len, per_row_mod):
    t = x.shape[0]
    tm = min(PROJ_TM, t)
    nblk = t // tm
    bps = seq_len // tm
    nkb = tm // ATT_T
    tok = lambda i: (i, 0)
    if per_row_mod:
        mod_spec = pl.BlockSpec((1, tm, D_MODEL), lambda i: (0, i, 0))
    else:
        mod_spec = pl.BlockSpec((1, 1, D_MODEL), lambda i: (i // bps, 0, 0))
    tspec = pl.BlockSpec((1, D_MODEL, tm), lambda i: (i // bps, 0, i % bps))
    tbspec = pl.BlockSpec((1, nkb, D_MODEL, ATT_T), lambda i: (i // bps, i % bps, 0, 0))
    row_bf = jax.ShapeDtypeStruct((t, D_MODEL), BF16)
    t_f32 = jax.ShapeDtypeStruct((n_seq, D_MODEL, seq_len), F32)
    t_bf = jax.ShapeDtypeStruct((n_seq, seq_len // ATT_T, D_MODEL, ATT_T), BF16)
    return pl.pallas_call(
        _proj_kernel,
        grid=(nblk,),
        in_specs=[
            pl.BlockSpec((tm, D_MODEL), tok),
            mod_spec, mod_spec,
            pl.BlockSpec((1, D_MODEL), lambda i: (0, 0)),
            pl.BlockSpec(w9.shape, lambda i: (0, 0, 0), pipeline_mode=pl.Buffered(1)),
        ],
        out_specs=[
            pl.BlockSpec((tm, D_MODEL), tok),
            tspec, tbspec,
            tspec, tbspec,
            pl.BlockSpec((tm, D_MODEL), tok),
            pl.BlockSpec((tm, D_MODEL), tok),
            pl.BlockSpec((tm, D_MODEL), tok),
            pl.BlockSpec((tm, D_MODEL), tok),
            pl.BlockSpec((tm, D_MODEL), tok),
        ],
        out_shape=[row_bf, t_f32, t_bf, t_f32, t_bf, row_bf,
                   jax.ShapeDtypeStruct((t, D_MODEL), F32), row_bf, row_bf, row_bf],
        scratch_shapes=[pltpu.VMEM((tm, D_MODEL), BF16)],
        compiler_params=pltpu.CompilerParams(
            dimension_semantics=("arbitrary",), vmem_limit_bytes=VMEM_LIMIT),
        name="proj",
    )(x, shift, scale, g, w9)


def _attn_prompt_kernel(bias_ref, q_ref, kt_ref, vt_ref, o_ref, oacc, acc):
    hg = pl.program_id(1)
    i = pl.program_id(2)
    t = ATT_T
    lane = lax.broadcasted_iota(jnp.int32, (t, LANES), 1)
    neg_upper = _neg_upper_incl(t)
    qpos = lax.broadcasted_iota(jnp.int32, (2 * t, t), 0) % t
    kpos = lax.broadcasted_iota(jnp.int32, (2 * t, t), 1)
    qms, biases = [], []
    for g in range(ATT_GROUPS):
        q2 = q_ref[:, g * LANES:(g + 1) * LANES].astype(F32)
        qms.append(jnp.concatenate([jnp.where(lane < HEAD_DIM, q2, 0.0),
                                    jnp.where(lane >= HEAD_DIM, q2, 0.0)], axis=0).astype(BF16))
        head = (hg * ATT_GROUPS + g) * 2
        biases.append((bias_ref[head], bias_ref[head + 1]))
    oacc[...] = jnp.zeros_like(oacc)
    acc[...] = jnp.zeros_like(acc)

    def step(g, j, masked):
        rows = pl.ds(g * LANES, LANES)
        s = jnp.dot(qms[g], kt_ref[0, j, rows, :], preferred_element_type=F32)
        z = jnp.concatenate([s[:t] + biases[g][0], s[t:] + biases[g][1]], axis=0)
        sp = _softplus(z)
        if masked:
            valid = kpos < qpos
            sp = jnp.where(valid, sp, 0.0)
        csum = jnp.dot(sp.astype(BF16), neg_upper, preferred_element_type=F32)
        a = acc[g]
        w = jnp.exp(z + csum + jnp.concatenate([a] * (t // LANES), axis=1))
        if masked:
            w = jnp.where(valid, w, 0.0)
        oacc[g] += lax.dot_general(w.astype(BF16), vt_ref[0, j, rows, :], NT_DIMS, preferred_element_type=F32)
        acc[g] = a + csum[:, 0:1]

    for g in range(ATT_GROUPS):
        step(g, i, True)

    def body(n, carry):
        j = i - 1 - 2 * n
        for g in range(ATT_GROUPS):
            step(g, j, False)
        for g in range(ATT_GROUPS):
            step(g, j - 1, False)
        return carry

    lax.fori_loop(0, i // 2, body, 0)

    @pl.when(i % 2 == 1)
    def _():
        for g in range(ATT_GROUPS):
            step(g, 0, False)

    for g in range(ATT_GROUPS):
        o = oacc[g]
        o_ref[:, g * LANES:(g + 1) * LANES] = jnp.where(lane < HEAD_DIM, o[:t], o[t:]).astype(BF16)


def _attn_prompt(bias, qb, ktb, vtb, n_seq, seq_len):
    t = ATT_T
    nq = seq_len // t
    width = ATT_GROUPS * LANES
    ngrp = D_MODEL // width
    kv_spec = pl.BlockSpec((1, nq, width, t), lambda b, hg, i, *_: (b, 0, hg, 0))
    q_spec = pl.BlockSpec((t, width), lambda b, hg, i, *_: (b * nq + i, hg))
    return pl.pallas_call(
        _attn_prompt_kernel,
        grid_spec=pltpu.PrefetchScalarGridSpec(
            num_scalar_prefetch=1,
            grid=(n_seq, ngrp, nq),
            in_specs=[q_spec, kv_spec, kv_spec],
            out_specs=q_spec,
            scratch_shapes=[pltpu.VMEM((ATT_GROUPS, 2 * t, LANES), F32),
                            pltpu.VMEM((ATT_GROUPS, 2 * t, LANES), F32)],
        ),
        out_shape=jax.ShapeDtypeStruct((n_seq * seq_len, D_MODEL), BF16),
        compiler_params=pltpu.CompilerParams(
            dimension_semantics=("arbitrary", "arbitrary", "arbitrary"), vmem_limit_bytes=VMEM_LIMIT),
        name="attn_prompt",
    )(bias, qb, ktb, vtb)


def _attn_sample_kernel(pt_ref, q_ref, kn_ref, vn_ref, bias_ref, *refs):
    del pt_ref
    p = SAMPLE_PAGES
    k_refs = refs[:p]
    v_refs = refs[p:2 * p]
    o_ref = refs[2 * p]
    qbd, oacc, acc, kcat, vcat = refs[2 * p + 1:]
    c = pl.program_id(1)
    nrow = q_ref.shape[1] * N_HEADS
    n_new = q_ref.shape[1]
    row = lax.broadcasted_iota(jnp.int32, (N_HEADS, D_MODEL), 0)
    col_head = lax.broadcasted_iota(jnp.int32, (N_HEADS, D_MODEL), 1) // HEAD_DIM
    head_mask = row == col_head
    upper = _strict_upper(PAGE)
    bias = bias_ref[...]

    def fold(z, w_to_out, valid=None):
        z = z + bias
        l1m = _neg_softplus(z)
        if valid is not None:
            l1m = jnp.where(valid, l1m, 0.0)
        suffix = jnp.dot(l1m.astype(BF16), upper, preferred_element_type=F32)
        a = acc[...]
        w = jnp.exp((z + l1m) + suffix + a)
        if valid is not None:
            w = jnp.where(valid, w, 0.0)
        oacc[...] += w_to_out(w.astype(BF16))
        acc[...] = a + jnp.sum(l1m, axis=-1, keepdims=True)

    @pl.when(c == 0)
    def _():
        q4 = q_ref[0]
        qbd[...] = jnp.concatenate(
            [jnp.where(head_mask, jnp.broadcast_to(q4[t:t + 1, :], (N_HEADS, D_MODEL)), 0.0)
             for t in range(n_new)], axis=0).astype(BF16)
        oacc[...] = jnp.zeros_like(oacc)
        acc[...] = jnp.zeros_like(acc)
        pad = jnp.zeros((PAGE - kn_ref.shape[1], D_MODEL), F32)
        kn = jnp.concatenate([kn_ref[0], pad], axis=0).astype(BF16)
        vn = jnp.concatenate([vn_ref[0], pad], axis=0).astype(BF16)
        z = lax.dot_general(qbd[...], kn, NT_DIMS, preferred_element_type=F32)
        qtok = lax.broadcasted_iota(jnp.int32, (nrow, PAGE), 0) // N_HEADS
        ktok = lax.broadcasted_iota(jnp.int32, (nrow, PAGE), 1)
        fold(z, lambda w: jnp.dot(w, vn, preferred_element_type=F32), valid=ktok < qtok)

    for r in range(p):
        kcat[:, r * PAGE:(r + 1) * PAGE] = k_refs[r][0].astype(BF16)
        vcat[:, r * PAGE:(r + 1) * PAGE] = v_refs[r][0].astype(BF16)
    blk = SAMPLE_BLOCK
    nblk = p * PAGE // blk
    z = jnp.dot(qbd[...], kcat[...], preferred_element_type=F32) + jnp.concatenate([bias] * p, axis=1)
    sp = _softplus(z)
    sp_rows = jnp.concatenate([sp[:, b * blk:(b + 1) * blk] for b in range(nblk)], axis=0).astype(BF16)
    csum = jnp.dot(sp_rows, _neg_upper_incl(blk), preferred_element_type=F32)
    a = acc[...]
    ws = [None] * nblk
    for b in reversed(range(nblk)):
        cb = csum[b * nrow:(b + 1) * nrow]
        ws[b] = jnp.exp(z[:, b * blk:(b + 1) * blk] + cb + jnp.concatenate([a] * (blk // LANES), axis=1))
        a = a + cb[:, 0:1]
    acc[...] = a
    w = jnp.concatenate(ws, axis=1).astype(BF16)
    oacc[...] += lax.dot_general(w, vcat[...], NT_DIMS, preferred_element_type=F32)

    @pl.when(c == pl.num_programs(1) - 1)
    def _():
        o = oacc[...]
        for t in range(n_new):
            blk = jnp.where(head_mask, o[t * N_HEADS:(t + 1) * N_HEADS], 0.0)
            o_ref[0, pl.ds(t, 1), :] = jnp.sum(blk, axis=0, keepdims=True)


def _attn_sample(pt_t, q, kn, vn, bias_rows, kt_pages, vt_pages):
    n_pages, n_seq = pt_t.shape
    n_new = q.shape[1]
    p = SAMPLE_PAGES
    nch = n_pages // p
    nrow = n_new * N_HEADS

    def page_spec(r):
        return pl.BlockSpec((1, D_MODEL, PAGE), lambda n, c, pt: (pt[(nch - 1 - c) * p + r, n], 0, 0))

    per_seq = lambda rows: pl.BlockSpec((1, rows, D_MODEL), lambda n, c, pt: (n, 0, 0))
    return pl.pallas_call(
        _attn_sample_kernel,
        grid_spec=pltpu.PrefetchScalarGridSpec(
            num_scalar_prefetch=1,
            grid=(n_seq, nch),
            in_specs=[per_seq(n_new), per_seq(kn.shape[1]), per_seq(vn.shape[1]),
                      pl.BlockSpec((nrow, LANES), lambda n, c, pt: (0, 0))]
                     + [page_spec(r) for r in range(p)] + [page_spec(r) for r in range(p)],
            out_specs=per_seq(n_new),
            scratch_shapes=[pltpu.VMEM((nrow, D_MODEL), BF16), pltpu.VMEM((nrow, D_MODEL), F32),
                            pltpu.VMEM((nrow, LANES), F32),
                            pltpu.VMEM((D_MODEL, p * PAGE), BF16), pltpu.VMEM((D_MODEL, p * PAGE), BF16)],
        ),
        out_shape=jax.ShapeDtypeStruct((n_seq, n_new, D_MODEL), F32),
        compiler_params=pltpu.CompilerParams(
            dimension_semantics=("arbitrary", "arbitrary"), vmem_limit_bytes=VMEM_LIMIT),
        name="attn_sample",
    )(pt_t, q, kn, vn, bias_rows, *([kt_pages] * p), *([vt_pages] * p))


def _conv_sample_kernel(state_ref, glu_ref, w_ref, b_ref, o_ref):
    n_state = state_ref.shape[0]
    n_new = glu_ref.shape[0]
    for t in range(n_new):
        a = jnp.broadcast_to(b_ref[...], o_ref.shape[1:])
        for j in range(CONV_K):
            i = t + j
            src = state_ref[i] if i < n_state else glu_ref[i - n_state]
            a = a + w_ref[pl.ds(j, 1), :] * src
        o_ref[t] = a


def _conv_sample(state, glu, w_dw, b_dw):
    n_state, n, ch = state.shape
    n_new = glu.shape[0]
    cw = LANES
    return pl.pallas_call(
        _conv_sample_kernel,
        grid=(ch // cw,),
        in_specs=[pl.BlockSpec((n_state, n, cw), lambda c: (0, 0, c)),
                  pl.BlockSpec((n_new, n, cw), lambda c: (0, 0, c)),
                  pl.BlockSpec((CONV_K, cw), lambda c: (0, c)),
                  pl.BlockSpec((1, cw), lambda c: (0, c))],
        out_specs=pl.BlockSpec((n_new, n, cw), lambda c: (0, 0, c)),
        out_shape=jax.ShapeDtypeStruct((n_new, n, ch), F32),
        compiler_params=pltpu.CompilerParams(dimension_semantics=("arbitrary",), vmem_limit_bytes=VMEM_LIMIT),
        name="conv_sample",
    )(state, glu, w_dw, b_dw)


def _finish_core(x, gate, att, sga, conv, sgc, sma, smc, lng, lnb, wa, wc, wo, gf):
    mu = jnp.mean(conv, axis=-1, keepdims=True)
    d = conv - mu
    ln = d * lax.rsqrt(jnp.mean(d * d, axis=-1, keepdims=True) + EPS) * lng + lnb
    cv = (_silu(ln) * sgc.astype(F32)).astype(BF16)
    av = (att.astype(F32) * sga.astype(F32)).astype(BF16)
    y_a = jnp.dot(av, wa, preferred_element_type=F32)
    y_c = jnp.dot(cv, wc, preferred_element_type=F32)
    merged = (sma.astype(F32) * y_a + smc.astype(F32) * y_c).astype(BF16)
    y = x + gate * jnp.dot(merged, wo, preferred_element_type=F32)
    return y * lax.rsqrt(jnp.mean(y * y, axis=-1, keepdims=True) + EPS) * gf


def _finish_prompt_kernel(x_ref, gate_ref, att_ref, sga_ref, glu_ref, halo_ref, sgc_ref, sma_ref, smc_ref,
                          wdw_ref, bdw_ref, lng_ref, lnb_ref, wa_ref, wc_ref, wo_ref, gf_ref, o_ref,
                          buf, shifted, conv_scr, *, blocks_per_seq):
    tm = x_ref.shape[0]
    first = (pl.program_id(0) % blocks_per_seq) == 0
    halo = halo_ref[...]
    buf[:HALO] = jnp.where(first, jnp.zeros_like(halo), halo)
    buf[HALO:] = glu_ref[...]
    base = HALO - (CONV_K - 1)
    srows = tm + HALO - SUBLANES
    for c in range(D_MODEL // LANES):
        cols = pl.ds(c * LANES, LANES)
        for sub in range(1, SUBLANES):
            shifted[sub - 1, :, cols] = buf[pl.ds(sub, srows), cols]
        acc = jnp.broadcast_to(bdw_ref[:, cols], (tm, LANES))
        for j in range(CONV_K):
            sub, start = (base + j) % SUBLANES, (base + j) // SUBLANES * SUBLANES
            src = buf[pl.ds(start, tm), cols] if sub == 0 else shifted[sub - 1, pl.ds(start, tm), cols]
            acc = acc + wdw_ref[pl.ds(j, 1), cols] * src
        conv_scr[:, cols] = acc
    conv = conv_scr[...]
    o_ref[...] = _finish_core(x_ref[...], gate_ref[0], att_ref[...], sga_ref[...], conv, sgc_ref[...],
                              sma_ref[...], smc_ref[...], lng_ref[...], lnb_ref[...], wa_ref[...],
                              wc_ref[...], wo_ref[...], gf_ref[...])


def _finish_sample_kernel(x_ref, gate_ref, att_ref, sga_ref, conv_ref, sgc_ref, sma_ref, smc_ref,
                          lng_ref, lnb_ref, wa_ref, wc_ref, wo_ref, gf_ref, o_ref):
    o_ref[...] = _finish_core(x_ref[...], gate_ref[0], att_ref[...], sga_ref[...], conv_ref[...], sgc_ref[...],
                              sma_ref[...], smc_ref[...], lng_ref[...], lnb_ref[...], wa_ref[...],
                              wc_ref[...], wo_ref[...], gf_ref[...])


def _const_spec(shape):
    return pl.BlockSpec(shape, lambda i: (0,) * len(shape))


def _finish_prompt(x, gate, att, sga, glu, sgc, sma, smc, wdw, bdw, lng, lnb, wa, wc, wo, gf, seq_len):
    t = x.shape[0]
    tm = FIN_TM
    bps = seq_len // tm
    tok = pl.BlockSpec((tm, D_MODEL), lambda i: (i, 0))
    vec = _const_spec((1, D_MODEL))
    mat = _const_spec((D_MODEL, D_MODEL))
    halo_spec = pl.BlockSpec((HALO, D_MODEL), lambda i: (jnp.maximum(i * (tm // HALO) - 1, 0), 0))
    return pl.pallas_call(
        functools.partial(_finish_prompt_kernel, blocks_per_seq=bps),
        grid=(t // tm,),
        in_specs=[tok, pl.BlockSpec((1, 1, D_MODEL), lambda i: (i // bps, 0, 0)), tok, tok, tok, halo_spec,
                  tok, tok, tok, _const_spec((CONV_K, D_MODEL)), vec, vec, vec, mat, mat, mat, vec],
        out_specs=tok,
        out_shape=jax.ShapeDtypeStruct((t, D_MODEL), F32),
        scratch_shapes=[pltpu.VMEM((tm + HALO, D_MODEL), F32),
                        pltpu.VMEM((SUBLANES - 1, tm + HALO - SUBLANES, D_MODEL), F32),
                        pltpu.VMEM((tm, D_MODEL), F32)],
        compiler_params=pltpu.CompilerParams(dimension_semantics=("arbitrary",), vmem_limit_bytes=VMEM_LIMIT),
        name="finish_prompt",
    )(x, gate, att, sga, glu, glu, sgc, sma, smc, wdw, bdw, lng, lnb, wa, wc, wo, gf)


def _finish_sample(x, gate, att, sga, conv, sgc, sma, smc, lng, lnb, wa, wc, wo, gf):
    t = x.shape[0]
    tok = _const_spec((t, D_MODEL))
    vec = _const_spec((1, D_MODEL))
    mat = _const_spec((D_MODEL, D_MODEL))
    return pl.pallas_call(
        _finish_sample_kernel,
        grid=(1,),
        in_specs=[tok, _const_spec((1, t, D_MODEL)), tok, tok, tok, tok, tok, tok, vec, vec, mat, mat, mat, vec],
        out_specs=tok,
        out_shape=jax.ShapeDtypeStruct((t, D_MODEL), F32),
        compiler_params=pltpu.CompilerParams(dimension_semantics=("arbitrary",), vmem_limit_bytes=VMEM_LIMIT),
        name="finish_sample",
    )(x, gate, att, sga, conv, sgc, sma, smc, lng, lnb, wa, wc, wo, gf)


def kernel(x_prompt, x_sample, cache_k, cache_v, state_conv, page_table, c_prompt, c_sample, w_ada, b_ada, g_norm, w_in, b_sb, w_dw, b_dw, ln_g, ln_b, w_a, w_c, w_out, g_final):
    assert w_in.shape[0] == 1, "one layer"
    n_p, s_p, d = x_prompt.shape
    n_s, s_s, _ = x_sample.shape
    n_state = state_conv.shape[2]
    half = d // 2

    w = w_in[0]
    grp = [w[:, g * d:(g + 1) * d] for g in range(9)]
    w9 = jnp.stack([
        grp[0], grp[1].T, grp[2].T, grp[3],
        jnp.concatenate([grp[4][:, :half], grp[5][:, :half]], axis=1),
        jnp.concatenate([grp[4][:, half:], grp[5][:, half:]], axis=1),
        grp[6], grp[7], grp[8]]).astype(BF16)
    wa, wc, wo = w_a[0].astype(BF16), w_c[0].astype(BF16), w_out[0].astype(BF16)
    g1, gf = g_norm[0][None, :], g_final[None, :]
    lng, lnb, bdw, wdw = ln_g[0][None, :], ln_b[0][None, :], b_dw[0][None, :], w_dw[0]
    bias = b_sb[0]

    mod = _mod(jnp.concatenate([c_prompt, c_sample], axis=0), w_ada[0].astype(BF16), b_ada[0][None, :])
    mod_p = mod[:n_p].reshape(n_p, 3, d)
    mod_s = mod[n_p:].reshape(n_s, 3, d)
    mod_s_rows = jnp.broadcast_to(mod_s[None], (s_s, n_s, 3, d)).reshape(s_s * n_s, 3, d)

    xp = x_prompt.reshape(n_p * s_p, d)
    shift_p, scale_p, gate_p = (mod_p[:, c][:, None, :] for c in range(3))
    qb, kt, ktb, vt, vtb, sga, glu, sgc, sma, smc = _proj(xp, shift_p, scale_p, g1, w9, n_p, s_p, False)
    att = _attn_prompt(bias, qb, ktb, vtb, n_p, s_p)
    y_p = _finish_prompt(xp, gate_p, att, sga, glu, sgc, sma, smc, wdw, bdw, lng, lnb, wa, wc, wo, gf, s_p)
    y_prompt = y_p.reshape(n_p, s_p, d)
    to_heads_t = lambda a: a.reshape(a.shape[0], N_HEADS, HEAD_DIM, a.shape[2]).transpose(0, 3, 1, 2)[None]
    k_prompt, v_prompt = to_heads_t(kt), to_heads_t(vt)
    conv_prompt = glu.reshape(n_p, s_p, d)[:, s_p - (CONV_K - 1):][None]

    xs = x_sample.transpose(1, 0, 2).reshape(s_s * n_s, d)
    shift_s, scale_s, gate_s = (mod_s_rows[:, c][None] for c in range(3))
    qb_s, kt_s, _, vt_s, _, sga_s, glu_s, sgc_s, sma_s, smc_s = _proj(
        xs, shift_s, scale_s, g1, w9, 1, s_s * n_s, True)
    tok_major = lambda a_t: a_t[0].T.reshape(s_s, n_s, d)
    k_new, v_new = tok_major(kt_s), tok_major(vt_s)
    seq_major = lambda a: a.transpose(1, 0, 2)
    pad_rows = lambda a: jnp.pad(a, ((0, 0), (0, SUBLANES - s_s), (0, 0)))
    q_seq = seq_major(qb_s.astype(F32).reshape(s_s, n_s, d))
    pages_t = lambda c: c[0].transpose(0, 2, 3, 1).reshape(c.shape[1], d, PAGE)
    bias_rows = jnp.broadcast_to(jnp.tile(bias, s_s)[:, None], (s_s * N_HEADS, LANES))
    att_s = _attn_sample(page_table.T, q_seq, pad_rows(seq_major(k_new)), pad_rows(seq_major(v_new)),
                         bias_rows, pages_t(cache_k), pages_t(cache_v))
    att_s = seq_major(att_s).reshape(s_s * n_s, d).astype(BF16)
    state_t = state_conv[0].transpose(1, 0, 2)
    glu_s3 = glu_s.reshape(s_s, n_s, d)
    conv_s = _conv_sample(state_t, glu_s3, wdw, bdw).reshape(s_s * n_s, d)
    y_s = _finish_sample(xs, gate_s, att_s, sga_s, conv_s, sgc_s, sma_s, smc_s, lng, lnb, wa, wc, wo, gf)
    y_sample = seq_major(y_s.reshape(s_s, n_s, d))
    heads = lambda a: seq_major(a).reshape(n_s, s_s, N_HEADS, HEAD_DIM)[None]
    k_sample, v_sample = heads(k_new), heads(v_new)
    conv_sample = seq_major(jnp.concatenate([state_t, glu_s3], axis=0)[s_s:])[None]
    assert n_state == CONV_K - 1
    return (y_prompt, y_sample, k_prompt, v_prompt, conv_prompt, k_sample, v_sample, conv_sample)
```

```python
import functools

import jax
import jax.numpy as jnp
from jax import lax
from jax.experimental import pallas as pl
from jax.experimental.pallas import tpu as pltpu

F32 = jnp.float32
BF16 = jnp.bfloat16

D_MODEL = 1024
N_HEADS = 16
HEAD_DIM = 64
CONV_K = 31
PAGE = 128
EPS = 1e-6
SB_SCALE = 0.125
LOG2E = 1.4426950408889634

LANES = 128
SUBLANES = 8
VMEM_LIMIT = 56 * 1024 * 1024

PROJ_TM = 256
ATT_T = 256
BIAS_PARTS = 3
ATT_UNROLL = 4
ATT_GROUPS = 4
FIN_TM = 256
HALO = 32
SAMPLE_PAGES = 16
SAMPLE_BLOCK = 256

NT_DIMS = (((1,), (1,)), ((), ()))


def _sigmoid(x):
    return 1.0 / (1.0 + jnp.exp(-x))


def _silu(x):
    return x * _sigmoid(x)


def _softplus(z):
    return jnp.maximum(z, 0.0) + jnp.log(1.0 + jnp.exp2(jnp.abs(z) * (-LOG2E)))


def _neg_softplus(z):
    return -_softplus(z)


def _strict_upper(n):
    r = lax.broadcasted_iota(jnp.int32, (n, n), 0)
    c = lax.broadcasted_iota(jnp.int32, (n, n), 1)
    return jnp.where(r > c, 1.0, 0.0).astype(BF16)


def _neg_upper_incl(n):
    r = lax.broadcasted_iota(jnp.int32, (n, n), 0)
    c = lax.broadcasted_iota(jnp.int32, (n, n), 1)
    return jnp.where(r >= c, -1.0, 0.0).astype(BF16)


def _mod_kernel(c_ref, w_ref, b_ref, o_ref):
    s = _silu(c_ref[...])
    o_ref[...] = jnp.dot(s.astype(BF16), w_ref[...], preferred_element_type=F32) + b_ref[...]


def _mod(c, w_bf, b):
    n = c.shape[0]
    return pl.pallas_call(
        _mod_kernel,
        out_shape=jax.ShapeDtypeStruct((n, 3 * D_MODEL), F32),
        compiler_params=pltpu.CompilerParams(vmem_limit_bytes=VMEM_LIMIT),
        name="mod",
    )(c, w_bf, b)


def _proj_kernel(x_ref, shift_ref, scale_ref, g_ref, w_ref,
                 q_ref, kt_ref, ktb_ref, vt_ref, vtb_ref, sga_ref, glu_ref, sgc_ref, sma_ref, smc_ref,
                 h_scr):
    nkb = ktb_ref.shape[1]
    half = D_MODEL // 2
    x = x_ref[...]
    y = x * lax.rsqrt(jnp.mean(x * x, axis=-1, keepdims=True) + EPS) * g_ref[...]
    h_scr[...] = (y * (1.0 + scale_ref[0]) + shift_ref[0]).astype(BF16)

    def nn(j):
        return jnp.dot(h_scr[...], w_ref[j], preferred_element_type=F32)

    def nt(j):
        return lax.dot_general(w_ref[j], h_scr[...], NT_DIMS, preferred_element_type=F32)

    def store_t(j, full_ref, blk_ref):
        r = nt(j)
        full_ref[0] = r
        rb = r.astype(BF16)
        for b in range(nkb):
            blk_ref[0, b] = rb[:, b * ATT_T:(b + 1) * ATT_T]

    q_ref[...] = (nn(0) * SB_SCALE).astype(BF16)
    store_t(1, kt_ref, ktb_ref)
    store_t(2, vt_ref, vtb_ref)
    sga_ref[...] = _silu(nn(3)).astype(BF16)
    for part in range(2):
        r = nn(4 + part)
        glu_ref[:, part * half:(part + 1) * half] = r[:, :half] * _sigmoid(r[:, half:])
    sgc_ref[...] = _silu(nn(6)).astype(BF16)
    sma_ref[...] = _sigmoid(nn(7)).astype(BF16)
    smc_ref[...] = _sigmoid(nn(8)).astype(BF16)


def _proj(x, shift, scale, g, w9, n_seq, seq_len, per_row_mod):
    t = x.shape[0]
    tm = min(PROJ_TM, t)
    nblk = t // tm
    bps = seq_len // tm
    nkb = tm // ATT_T
    tok = lambda i: (i, 0)
    if per_row_mod:
        mod_spec = pl.BlockSpec((1, tm, D_MODEL), lambda i: (0, i, 0))
    else:
        mod_spec = pl.BlockSpec((1, 1, D_MODEL), lambda i: (i // bps, 0, 0))
    tspec = pl.BlockSpec((1, D_MODEL, tm), lambda i: (i // bps, 0, i % bps))
    tbspec = pl.BlockSpec((1, nkb, D_MODEL, ATT_T), lambda i: (i // bps, i % bps, 0, 0))
    row_bf = jax.ShapeDtypeStruct((t, D_MODEL), BF16)
    t_f32 = jax.ShapeDtypeStruct((n_seq, D_MODEL, seq_len), F32)
    t_bf = jax.ShapeDtypeStruct((n_seq, seq_len // ATT_T, D_MODEL, ATT_T), BF16)
    return pl.pallas_call(
        _proj_kernel,
        grid=(nblk,),
        in_specs=[
            pl.BlockSpec((tm, D_MODEL), tok),
            mod_spec, mod_spec,
            pl.BlockSpec((1, D_MODEL), lambda i: (0, 0)),
            pl.BlockSpec(w9.shape, lambda i: (0, 0, 0), pipeline_mode=pl.Buffered(1)),
        ],
        out_specs=[
            pl.BlockSpec((tm, D_MODEL), tok),
            tspec, tbspec,
            tspec, tbspec,
            pl.BlockSpec((tm, D_MODEL), tok),
            pl.BlockSpec((tm, D_MODEL), tok),
            pl.BlockSpec((tm, D_MODEL), tok),
            pl.BlockSpec((tm, D_MODEL), tok),
            pl.BlockSpec((tm, D_MODEL), tok),
        ],
        out_shape=[row_bf, t_f32, t_bf, t_f32, t_bf, row_bf,
                   jax.ShapeDtypeStruct((t, D_MODEL), F32), row_bf, row_bf, row_bf],
        scratch_shapes=[pltpu.VMEM((tm, D_MODEL), BF16)],
        compiler_params=pltpu.CompilerParams(
            dimension_semantics=("arbitrary",), vmem_limit_bytes=VMEM_LIMIT),
        name="proj",
    )(x, shift, scale, g, w9)


def _attn_prompt_kernel(bias_ref, q_ref, kt_ref, vt_ref, o_ref, oacc, acc, ones_rows):
    hg = pl.program_id(1)
    i = pl.program_id(2)
    t = ATT_T
    lane = lax.broadcasted_iota(jnp.int32, (t, LANES), 1)
    neg_upper = _neg_upper_incl(t)
    qpos = lax.broadcasted_iota(jnp.int32, (2 * t, t), 0) % t
    kpos = lax.broadcasted_iota(jnp.int32, (2 * t, t), 1)
    ones_rows[...] = jnp.where(lax.broadcasted_iota(jnp.int32, (LANES, t), 0) < BIAS_PARTS, 1.0, 0.0).astype(BF16)
    row2 = lax.broadcasted_iota(jnp.int32, (2 * t, LANES), 0)
    lane2 = lax.broadcasted_iota(jnp.int32, (2 * t, LANES), 1)
    qms = []
    for g in range(ATT_GROUPS):
        q2 = q_ref[:, g * LANES:(g + 1) * LANES].astype(F32)
        qm = jnp.concatenate([jnp.where(lane < HEAD_DIM, q2, 0.0),
                              jnp.where(lane >= HEAD_DIM, q2, 0.0)], axis=0).astype(BF16)
        head = (hg * ATT_GROUPS + g) * 2
        rest = jnp.where(row2 < t, bias_ref[head], bias_ref[head + 1])
        bias_cols = jnp.zeros((2 * t, LANES), F32)
        for part in range(BIAS_PARTS):
            piece = rest.astype(BF16).astype(F32)
            bias_cols = jnp.where(lane2 == part, piece, bias_cols)
            rest = rest - piece
        qms.append(jnp.concatenate([qm, bias_cols.astype(BF16)], axis=1))
    oacc[...] = jnp.zeros_like(oacc)
    acc[...] = jnp.zeros_like(acc)

    def step(g, j, masked):
        rows = pl.ds(g * LANES, LANES)
        kt_aug = jnp.concatenate([kt_ref[0, j, rows, :], ones_rows[...]], axis=0)
        z = jnp.dot(qms[g], kt_aug, preferred_element_type=F32)
        sp = _softplus(z)
        if masked:
            valid = kpos < qpos
            sp = jnp.where(valid, sp, 0.0)
        csum = jnp.dot(sp.astype(BF16), neg_upper, preferred_element_type=F32)
        a = acc[g]
        w = jnp.exp(z + csum + jnp.concatenate([a] * (t // LANES), axis=1))
        if masked:
            w = jnp.where(valid, w, 0.0)
        oacc[g] += lax.dot_general(w.astype(BF16), vt_ref[0, j, rows, :], NT_DIMS, preferred_element_type=F32)
        acc[g] = a + csum[:, 0:1]

    for g in range(ATT_GROUPS):
        step(g, i, True)

    def body(n, carry):
        j = i - 1 - ATT_UNROLL * n
        for u in range(ATT_UNROLL):
            for g in range(ATT_GROUPS):
                step(g, j - u, False)
        return carry

    lax.fori_loop(0, i // ATT_UNROLL, body, 0)

    def tail(n, carry):
        for g in range(ATT_GROUPS):
            step(g, i % ATT_UNROLL - 1 - n, False)
        return carry

    lax.fori_loop(0, i % ATT_UNROLL, tail, 0)

    for g in range(ATT_GROUPS):
        o = oacc[g]
        o_ref[:, g * LANES:(g + 1) * LANES] = jnp.where(lane < HEAD_DIM, o[:t], o[t:]).astype(BF16)


def _attn_prompt(bias, qb, ktb, vtb, n_seq, seq_len):
    t = ATT_T
    nq = seq_len // t
    width = ATT_GROUPS * LANES
    ngrp = D_MODEL // width
    kv_spec = pl.BlockSpec((1, nq, width, t), lambda b, hg, i, *_: (b, 0, hg, 0))
    q_spec = pl.BlockSpec((t, width), lambda b, hg, i, *_: (b * nq + i, hg))
    return pl.pallas_call(
        _attn_prompt_kernel,
        grid_spec=pltpu.PrefetchScalarGridSpec(
            num_scalar_prefetch=1,
            grid=(n_seq, ngrp, nq),
            in_specs=[q_spec, kv_spec, kv_spec],
            out_specs=q_spec,
            scratch_shapes=[pltpu.VMEM((ATT_GROUPS, 2 * t, LANES), F32),
                            pltpu.VMEM((ATT_GROUPS, 2 * t, LANES), F32),
                            pltpu.VMEM((LANES, t), BF16)],
        ),
        out_shape=jax.ShapeDtypeStruct((n_seq * seq_len, D_MODEL), BF16),
        compiler_params=pltpu.CompilerParams(
            dimension_semantics=("arbitrary", "arbitrary", "arbitrary"), vmem_limit_bytes=VMEM_LIMIT),
        name="attn_prompt",
    )(bias, qb, ktb, vtb)


def _attn_sample_kernel(pt_ref, q_ref, kn_ref, vn_ref, bias_ref, *refs):
    del pt_ref
    p = SAMPLE_PAGES
    k_refs = refs[:p]
    v_refs = refs[p:2 * p]
    o_ref = refs[2 * p]
    qbd, oacc, acc, kcat, vcat = refs[2 * p + 1:]
    c = pl.program_id(1)
    nrow = q_ref.shape[1] * N_HEADS
    n_new = q_ref.shape[1]
    row = lax.broadcasted_iota(jnp.int32, (N_HEADS, D_MODEL), 0)
    col_head = lax.broadcasted_iota(jnp.int32, (N_HEADS, D_MODEL), 1) // HEAD_DIM
    head_mask = row == col_head
    upper = _strict_upper(PAGE)
    bias = bias_ref[...]

    def fold(z, w_to_out, valid=None):
        z = z + bias
        l1m = _neg_softplus(z)
        if valid is not None:
            l1m = jnp.where(valid, l1m, 0.0)
        suffix = jnp.dot(l1m.astype(BF16), upper, preferred_element_type=F32)
        a = acc[...]
        w = jnp.exp((z + l1m) + suffix + a)
        if valid is not None:
            w = jnp.where(valid, w, 0.0)
        oacc[...] += w_to_out(w.astype(BF16))
        acc[...] = a + jnp.sum(l1m, axis=-1, keepdims=True)

    @pl.when(c == 0)
    def _():
        q4 = q_ref[0]
        qbd[...] = jnp.concatenate(
            [jnp.where(head_mask, jnp.broadcast_to(q4[t:t + 1, :], (N_HEADS, D_MODEL)), 0.0)
             for t in range(n_new)], axis=0).astype(BF16)
        oacc[...] = jnp.zeros_like(oacc)
        acc[...] = jnp.zeros_like(acc)
        pad = jnp.zeros((PAGE - kn_ref.shape[1], D_MODEL), F32)
        kn = jnp.concatenate([kn_ref[0], pad], axis=0).astype(BF16)
        vn = jnp.concatenate([vn_ref[0], pad], axis=0).astype(BF16)
        z = lax.dot_general(qbd[...], kn, NT_DIMS, preferred_element_type=F32)
        qtok = lax.broadcasted_iota(jnp.int32, (nrow, PAGE), 0) // N_HEADS
        ktok = lax.broadcasted_iota(jnp.int32, (nrow, PAGE), 1)
        fold(z, lambda w: jnp.dot(w, vn, preferred_element_type=F32), valid=ktok < qtok)

    for r in range(p):
        kcat[:, r * PAGE:(r + 1) * PAGE] = k_refs[r][0].astype(BF16)
        vcat[:, r * PAGE:(r + 1) * PAGE] = v_refs[r][0].astype(BF16)
    blk = SAMPLE_BLOCK
    nblk = p * PAGE // blk
    z = jnp.dot(qbd[...], kcat[...], preferred_element_type=F32) + jnp.concatenate([bias] * p, axis=1)
    sp = _softplus(z)
    sp_rows = jnp.concatenate([sp[:, b * blk:(b + 1) * blk] for b in range(nblk)], axis=0).astype(BF16)
    csum = jnp.dot(sp_rows, _neg_upper_incl(blk), preferred_element_type=F32)
    a = acc[...]
    ws = [None] * nblk
    for b in reversed(range(nblk)):
        cb = csum[b * nrow:(b + 1) * nrow]
        ws[b] = jnp.exp(z[:, b * blk:(b + 1) * blk] + cb + jnp.concatenate([a] * (blk // LANES), axis=1))
        a = a + cb[:, 0:1]
    acc[...] = a
    w = jnp.concatenate(ws, axis=1).astype(BF16)
    oacc[...] += lax.dot_general(w, vcat[...], NT_DIMS, preferred_element_type=F32)

    @pl.when(c == pl.num_programs(1) - 1)
    def _():
        o = oacc[...]
        for t in range(n_new):
            blk = jnp.where(head_mask, o[t * N_HEADS:(t + 1) * N_HEADS], 0.0)
            o_ref[0, pl.ds(t, 1), :] = jnp.sum(blk, axis=0, keepdims=True)


def _attn_sample(pt_t, q, kn, vn, bias_rows, kt_pages, vt_pages):
    n_pages, n_seq = pt_t.shape
    n_new = q.shape[1]
    p = SAMPLE_PAGES
    nch = n_pages // p
    nrow = n_new * N_HEADS

    def page_spec(r):
        return pl.BlockSpec((1, D_MODEL, PAGE), lambda n, c, pt: (pt[(nch - 1 - c) * p + r, n], 0, 0))

    per_seq = lambda rows: pl.BlockSpec((1, rows, D_MODEL), lambda n, c, pt: (n, 0, 0))
    return pl.pallas_call(
        _attn_sample_kernel,
        grid_spec=pltpu.PrefetchScalarGridSpec(
            num_scalar_prefetch=1,
            grid=(n_seq, nch),
            in_specs=[per_seq(n_new), per_seq(kn.shape[1]), per_seq(vn.shape[1]),
                      pl.BlockSpec((nrow, LANES), lambda n, c, pt: (0, 0))]
                     + [page_spec(r) for r in range(p)] + [page_spec(r) for r in range(p)],
            out_specs=per_seq(n_new),
            scratch_shapes=[pltpu.VMEM((nrow, D_MODEL), BF16), pltpu.VMEM((nrow, D_MODEL), F32),
                            pltpu.VMEM((nrow, LANES), F32),
                            pltpu.VMEM((D_MODEL, p * PAGE), BF16), pltpu.VMEM((D_MODEL, p * PAGE), BF16)],
        ),
        out_shape=jax.ShapeDtypeStruct((n_seq, n_new, D_MODEL), F32),
        compiler_params=pltpu.CompilerParams(
            dimension_semantics=("arbitrary", "arbitrary"), vmem_limit_bytes=VMEM_LIMIT),
        name="attn_sample",
    )(pt_t, q, kn, vn, bias_rows, *([kt_pages] * p), *([vt_pages] * p))


def _conv_sample_kernel(state_ref, glu_ref, w_ref, b_ref, o_ref):
    n_state = state_ref.shape[0]
    n_new = glu_ref.shape[0]
    for t in range(n_new):
        a = jnp.broadcast_to(b_ref[...], o_ref.shape[1:])
        for j in range(CONV_K):
            i = t + j
            src = state_ref[i] if i < n_state else glu_ref[i - n_state]
            a = a + w_ref[pl.ds(j, 1), :] * src
        o_ref[t] = a


def _conv_sample(state, glu, w_dw, b_dw):
    n_state, n, ch = state.shape
    n_new = glu.shape[0]
    cw = LANES
    return pl.pallas_call(
        _conv_sample_kernel,
        grid=(ch // cw,),
        in_specs=[pl.BlockSpec((n_state, n, cw), lambda c: (0, 0, c)),
                  pl.BlockSpec((n_new, n, cw), lambda c: (0, 0, c)),
                  pl.BlockSpec((CONV_K, cw), lambda c: (0, c)),
                  pl.BlockSpec((1, cw), lambda c: (0, c))],
        out_specs=pl.BlockSpec((n_new, n, cw), lambda c: (0, 0, c)),
        out_shape=jax.ShapeDtypeStruct((n_new, n, ch), F32),
        compiler_params=pltpu.CompilerParams(dimension_semantics=("arbitrary",), vmem_limit_bytes=VMEM_LIMIT),
        name="conv_sample",
    )(state, glu, w_dw, b_dw)


def _finish_core(x, gate, att, sga, conv, sgc, sma, smc, lng, lnb, wa, wc, wo, gf):
    mu = jnp.mean(conv, axis=-1, keepdims=True)
    d = conv - mu
    ln = d * lax.rsqrt(jnp.mean(d * d, axis=-1, keepdims=True) + EPS) * lng + lnb
    cv = (_silu(ln) * sgc.astype(F32)).astype(BF16)
    av = (att.astype(F32) * sga.astype(F32)).astype(BF16)
    y_a = jnp.dot(av, wa, preferred_element_type=F32)
    y_c = jnp.dot(cv, wc, preferred_element_type=F32)
    merged = (sma.astype(F32) * y_a + smc.astype(F32) * y_c).astype(BF16)
    y = x + gate * jnp.dot(merged, wo, preferred_element_type=F32)
    return y * lax.rsqrt(jnp.mean(y * y, axis=-1, keepdims=True) + EPS) * gf


def _finish_prompt_kernel(x_ref, gate_ref, att_ref, sga_ref, glu_ref, halo_ref, sgc_ref, sma_ref, smc_ref,
                          wdw_ref, bdw_ref, lng_ref, lnb_ref, wa_ref, wc_ref, wo_ref, gf_ref, o_ref,
                          buf, shifted, conv_scr, *, blocks_per_seq):
    tm = x_ref.shape[0]
    first = (pl.program_id(0) % blocks_per_seq) == 0
    halo = halo_ref[...]
    buf[:HALO] = jnp.where(first, jnp.zeros_like(halo), halo)
    buf[HALO:] = glu_ref[...]
    base = HALO - (CONV_K - 1)
    srows = tm + HALO - SUBLANES
    for c in range(D_MODEL // LANES):
        cols = pl.ds(c * LANES, LANES)
        for sub in range(1, SUBLANES):
            shifted[sub - 1, :, cols] = buf[pl.ds(sub, srows), cols]
        acc = jnp.broadcast_to(bdw_ref[:, cols], (tm, LANES))
        for j in range(CONV_K):
            sub, start = (base + j) % SUBLANES, (base + j) // SUBLANES * SUBLANES
            src = buf[pl.ds(start, tm), cols] if sub == 0 else shifted[sub - 1, pl.ds(start, tm), cols]
            acc = acc + wdw_ref[pl.ds(j, 1), cols] * src
        conv_scr[:, cols] = acc
    conv = conv_scr[...]
    o_ref[...] = _finish_core(x_ref[...], gate_ref[0], att_ref[...], sga_ref[...], conv, sgc_ref[...],
                              sma_ref[...], smc_ref[...], lng_ref[...], lnb_ref[...], wa_ref[...],
                              wc_ref[...], wo_ref[...], gf_ref[...])


def _finish_sample_kernel(x_ref, gate_ref, att_ref, sga_ref, conv_ref, sgc_ref, sma_ref, smc_ref,
                          lng_ref, lnb_ref, wa_ref, wc_ref, wo_ref, gf_ref, o_ref):
    o_ref[...] = _finish_core(x_ref[...], gate_ref[0], att_ref[...], sga_ref[...], conv_ref[...], sgc_ref[...],
                              sma_ref[...], smc_ref[...], lng_ref[...], lnb_ref[...], wa_ref[...],
                              wc_ref[...], wo_ref[...], gf_ref[...])


def _const_spec(shape):
    return pl.BlockSpec(shape, lambda i: (0,) * len(shape))


def _finish_prompt(x, gate, att, sga, glu, sgc, sma, smc, wdw, bdw, lng, lnb, wa, wc, wo, gf, seq_len):
    t = x.shape[0]
    tm = FIN_TM
    bps = seq_len // tm
    tok = pl.BlockSpec((tm, D_MODEL), lambda i: (i, 0))
    vec = _const_spec((1, D_MODEL))
    mat = _const_spec((D_MODEL, D_MODEL))
    halo_spec = pl.BlockSpec((HALO, D_MODEL), lambda i: (jnp.maximum(i * (tm // HALO) - 1, 0), 0))
    return pl.pallas_call(
        functools.partial(_finish_prompt_kernel, blocks_per_seq=bps),
        grid=(t // tm,),
        in_specs=[tok, pl.BlockSpec((1, 1, D_MODEL), lambda i: (i // bps, 0, 0)), tok, tok, tok, halo_spec,
                  tok, tok, tok, _const_spec((CONV_K, D_MODEL)), vec, vec, vec, mat, mat, mat, vec],
        out_specs=tok,
        out_shape=jax.ShapeDtypeStruct((t, D_MODEL), F32),
        scratch_shapes=[pltpu.VMEM((tm + HALO, D_MODEL), F32),
                        pltpu.VMEM((SUBLANES - 1, tm + HALO - SUBLANES, D_MODEL), F32),
                        pltpu.VMEM((tm, D_MODEL), F32)],
        compiler_params=pltpu.CompilerParams(dimension_semantics=("arbitrary",), vmem_limit_bytes=VMEM_LIMIT),
        name="finish_prompt",
    )(x, gate, att, sga, glu, glu, sgc, sma, smc, wdw, bdw, lng, lnb, wa, wc, wo, gf)


def _finish_sample(x, gate, att, sga, conv, sgc, sma, smc, lng, lnb, wa, wc, wo, gf):
    t = x.shape[0]
    tok = _const_spec((t, D_MODEL))
    vec = _const_spec((1, D_MODEL))
    mat = _const_spec((D_MODEL, D_MODEL))
    return pl.pallas_call(
        _finish_sample_kernel,
        grid=(1,),
        in_specs=[tok, _const_spec((1, t, D_MODEL)), tok, tok, tok, tok, tok, tok, vec, vec, mat, mat, mat, vec],
        out_specs=tok,
        out_shape=jax.ShapeDtypeStruct((t, D_MODEL), F32),
        compiler_params=pltpu.CompilerParams(dimension_semantics=("arbitrary",), vmem_limit_bytes=VMEM_LIMIT),
        name="finish_sample",
    )(x, gate, att, sga, conv, sgc, sma, smc, lng, lnb, wa, wc, wo, gf)


def kernel(x_prompt, x_sample, cache_k, cache_v, state_conv, page_table, c_prompt, c_sample, w_ada, b_ada, g_norm, w_in, b_sb, w_dw, b_dw, ln_g, ln_b, w_a, w_c, w_out, g_final):
    assert w_in.shape[0] == 1, "one layer"
    n_p, s_p, d = x_prompt.shape
    n_s, s_s, _ = x_sample.shape
    n_state = state_conv.shape[2]
    half = d // 2

    w = w_in[0]
    grp = [w[:, g * d:(g + 1) * d] for g in range(9)]
    w9 = jnp.stack([
        grp[0], grp[1].T, grp[2].T, grp[3],
        jnp.concatenate([grp[4][:, :half], grp[5][:, :half]], axis=1),
        jnp.concatenate([grp[4][:, half:], grp[5][:, half:]], axis=1),
        grp[6], grp[7], grp[8]]).astype(BF16)
    wa, wc, wo = w_a[0].astype(BF16), w_c[0].astype(BF16), w_out[0].astype(BF16)
    g1, gf = g_norm[0][None, :], g_final[None, :]
    lng, lnb, bdw, wdw = ln_g[0][None, :], ln_b[0][None, :], b_dw[0][None, :], w_dw[0]
    bias = b_sb[0]

    mod = _mod(jnp.concatenate([c_prompt, c_sample], axis=0), w_ada[0].astype(BF16), b_ada[0][None, :])
    mod_p = mod[:n_p].reshape(n_p, 3, d)
    mod_s = mod[n_p:].reshape(n_s, 3, d)
    mod_s_rows = jnp.broadcast_to(mod_s[None], (s_s, n_s, 3, d)).reshape(s_s * n_s, 3, d)

    xp = x_prompt.reshape(n_p * s_p, d)
    shift_p, scale_p, gate_p = (mod_p[:, c][:, None, :] for c in range(3))
    qb, kt, ktb, vt, vtb, sga, glu, sgc, sma, smc = _proj(xp, shift_p, scale_p, g1, w9, n_p, s_p, False)
    att = _attn_prompt(bias, qb, ktb, vtb, n_p, s_p)
    y_p = _finish_prompt(xp, gate_p, att, sga, glu, sgc, sma, smc, wdw, bdw, lng, lnb, wa, wc, wo, gf, s_p)
    y_prompt = y_p.reshape(n_p, s_p, d)
    to_heads_t = lambda a: a.reshape(a.shape[0], N_HEADS, HEAD_DIM, a.shape[2]).transpose(0, 3, 1, 2)[None]
    k_prompt, v_prompt = to_heads_t(kt), to_heads_t(vt)
    conv_prompt = glu.reshape(n_p, s_p, d)[:, s_p - (CONV_K - 1):][None]

    xs = x_sample.transpose(1, 0, 2).reshape(s_s * n_s, d)
    shift_s, scale_s, gate_s = (mod_s_rows[:, c][None] for c in range(3))
    qb_s, kt_s, _, vt_s, _, sga_s, glu_s, sgc_s, sma_s, smc_s = _proj(
        xs, shift_s, scale_s, g1, w9, 1, s_s * n_s, True)
    tok_major = lambda a_t: a_t[0].T.reshape(s_s, n_s, d)
    k_new, v_new = tok_major(kt_s), tok_major(vt_s)
    seq_major = lambda a: a.transpose(1, 0, 2)
    pad_rows = lambda a: jnp.pad(a, ((0, 0), (0, SUBLANES - s_s), (0, 0)))
    q_seq = seq_major(qb_s.astype(F32).reshape(s_s, n_s, d))
    pages_t = lambda c: c[0].transpose(0, 2, 3, 1).reshape(c.shape[1], d, PAGE)
    bias_rows = jnp.broadcast_to(jnp.tile(bias, s_s)[:, None], (s_s * N_HEADS, LANES))
    att_s = _attn_sample(page_table.T, q_seq, pad_rows(seq_major(k_new)), pad_rows(seq_major(v_new)),
                         bias_rows, pages_t(cache_k), pages_t(cache_v))
    att_s = seq_major(att_s).reshape(s_s * n_s, d).astype(BF16)
    state_t = state_conv[0].transpose(1, 0, 2)
    glu_s3 = glu_s.reshape(s_s, n_s, d)
    conv_s = _conv_sample(state_t, glu_s3, wdw, bdw).reshape(s_s * n_s, d)
    y_s = _finish_sample(xs, gate_s, att_s, sga_s, conv_s, sgc_s, sma_s, smc_s, lng, lnb, wa, wc, wo, gf)
    y_sample = seq_major(y_s.reshape(s_s, n_s, d))
    heads = lambda a: seq_major(a).reshape(n_s, s_s, N_HEADS, HEAD_DIM)[None]
    k_sample, v_sample = heads(k_new), heads(v_new)
    conv_sample = seq_major(jnp.concatenate([state_t, glu_s3], axis=0)[s_s:])[None]
    assert n_state == CONV_K - 1
    return (y_prompt, y_sample, k_prompt, v_prompt, conv_prompt, k_sample, v_sample, conv_sample)
```

```python
import functools

import jax
import jax.numpy as jnp
from jax import lax
from jax.experimental import pallas as pl
from jax.experimental.pallas import tpu as pltpu

F32 = jnp.float32
BF16 = jnp.bfloat16

D_MODEL = 1024
N_HEADS = 16
HEAD_DIM = 64
CONV_K = 31
PAGE = 128
EPS = 1e-6
SB_SCALE = 0.125
LOG2E = 1.4426950408889634

LANES = 128
SUBLANES = 8
VMEM_LIMIT = 56 * 1024 * 1024

PROJ_TM = 256
ATT_T = 256
BIAS_PARTS = 3
ATT_UNROLL = 4
ATT_GROUPS = 4
FIN_TM = 256
HALO = 32
SAMPLE_PAGES = 16
SAMPLE_BLOCK = 256

NT_DIMS = (((1,), (1,)), ((), ()))


def _sigmoid(x):
    return 1.0 / (1.0 + jnp.exp(-x))


def _silu(x):
    return x * _sigmoid(x)


def _softplus(z):
    return jnp.maximum(z, 0.0) + jnp.log(1.0 + jnp.exp2(jnp.abs(z) * (-LOG2E)))


def _neg_softplus(z):
    return -_softplus(z)


def _strict_upper(n):
    r = lax.broadcasted_iota(jnp.int32, (n, n), 0)
    c = lax.broadcasted_iota(jnp.int32, (n, n), 1)
    return jnp.where(r > c, 1.0, 0.0).astype(BF16)


def _neg_upper_incl(n):
    r = lax.broadcasted_iota(jnp.int32, (n, n), 0)
    c = lax.broadcasted_iota(jnp.int32, (n, n), 1)
    return jnp.where(r >= c, -1.0, 0.0).astype(BF16)


def _mod_kernel(c_ref, w_ref, b_ref, o_ref):
    s = _silu(c_ref[...])
    o_ref[...] = jnp.dot(s.astype(BF16), w_ref[...], preferred_element_type=F32) + b_ref[...]


def _mod(c, w_bf, b):
    n = c.shape[0]
    return pl.pallas_call(
        _mod_kernel,
        out_shape=jax.ShapeDtypeStruct((n, 3 * D_MODEL), F32),
        compiler_params=pltpu.CompilerParams(vmem_limit_bytes=VMEM_LIMIT),
        name="mod",
    )(c, w_bf, b)


def _proj_kernel(x_ref, shift_ref, scale_ref, g_ref, w_ref,
                 q_ref, kt_ref, ktb_ref, vt_ref, vtb_ref, sga_ref, glu_ref, sgc_ref, sma_ref, smc_ref,
                 h_scr):
    nkb = ktb_ref.shape[1]
    half = D_MODEL // 2
    x = x_ref[...]
    y = x * lax.rsqrt(jnp.mean(x * x, axis=-1, keepdims=True) + EPS) * g_ref[...]
    h_scr[...] = (y * (1.0 + scale_ref[0]) + shift_ref[0]).astype(BF16)

    def nn(j):
        return jnp.dot(h_scr[...], w_ref[j], preferred_element_type=F32)

    def nt(j):
        return lax.dot_general(w_ref[j], h_scr[...], NT_DIMS, preferred_element_type=F32)

    def store_t(j, full_ref, blk_ref):
        r = nt(j)
        full_ref[0] = r
        rb = r.astype(BF16)
        for b in range(nkb):
            blk_ref[0, b] = rb[:, b * ATT_T:(b + 1) * ATT_T]

    q_ref[...] = (nn(0) * SB_SCALE).astype(BF16)
    store_t(1, kt_ref, ktb_ref)
    store_t(2, vt_ref, vtb_ref)
    sga_ref[...] = _silu(nn(3)).astype(BF16)
    for part in range(2):
        r = nn(4 + part)
        glu_ref[:, part * half:(part + 1) * half] = r[:, :half] * _sigmoid(r[:, half:])
    sgc_ref[...] = _silu(nn(6)).astype(BF16)
    sma_ref[...] = _sigmoid(nn(7)).astype(BF16)
    smc_ref[...] = _sigmoid(nn(8)).astype(BF16)


def _proj(x, shift, scale, g, w9, n_seq, seq_len, per_row_mod):
    t = x.shape[0]
    tm = min(PROJ_TM, t)
    nblk = t // tm
    bps = seq_len // tm
    nkb = tm // ATT_T
    tok = lambda i: (i, 0)
    if per_row_mod:
        mod_spec = pl.BlockSpec((1, tm, D_MODEL), lambda i: (0, i, 0))
    else:
        mod_spec = pl.BlockSpec((1, 1, D_MODEL), lambda i: (i // bps, 0, 0))
    tspec = pl.BlockSpec((1, D_MODEL, tm), lambda i: (i // bps, 0, i % bps))
    tbspec = pl.BlockSpec((1, nkb, D_MODEL, ATT_T), lambda i: (i // bps, i % bps, 0, 0))
    row_bf = jax.ShapeDtypeStruct((t, D_MODEL), BF16)
    t_f32 = jax.ShapeDtypeStruct((n_seq, D_MODEL, seq_len), F32)
    t_bf = jax.ShapeDtypeStruct((n_seq, seq_len // ATT_T, D_MODEL, ATT_T), BF16)
    return pl.pallas_call(
        _proj_kernel,
        grid=(nblk,),
        in_specs=[
            pl.BlockSpec((tm, D_MODEL), tok),
            mod_spec, mod_spec,
            pl.BlockSpec((1, D_MODEL), lambda i: (0, 0)),
            pl.BlockSpec(w9.shape, lambda i: (0, 0, 0), pipeline_mode=pl.Buffered(1)),
        ],
        out_specs=[
            pl.BlockSpec((tm, D_MODEL), tok),
            tspec, tbspec,
            tspec, tbspec,
            pl.BlockSpec((tm, D_MODEL), tok),
            pl.BlockSpec((tm, D_MODEL), tok),
            pl.BlockSpec((tm, D_MODEL), tok),
            pl.BlockSpec((tm, D_MODEL), tok),
            pl.BlockSpec((tm, D_MODEL), tok),
        ],
        out_shape=[row_bf, t_f32, t_bf, t_f32, t_bf, row_bf,
                   jax.ShapeDtypeStruct((t, D_MODEL), F32), row_bf, row_bf, row_bf],
        scratch_shapes=[pltpu.VMEM((tm, D_MODEL), BF16)],
        compiler_params=pltpu.CompilerParams(
            dimension_semantics=("arbitrary",), vmem_limit_bytes=VMEM_LIMIT),
        name="proj",
    )(x, shift, scale, g, w9)


def _attn_prompt_kernel(bias_ref, q_ref, kt_ref, vt_ref, o_ref, oacc, acc, ones_rows):
    hg = pl.program_id(1)
    i = pl.program_id(2)
    t = ATT_T
    lane = lax.broadcasted_iota(jnp.int32, (t, LANES), 1)
    neg_upper = _neg_upper_incl(t)
    qpos = lax.broadcasted_iota(jnp.int32, (2 * t, t), 0) % t
    kpos = lax.broadcasted_iota(jnp.int32, (2 * t, t), 1)
    ones_rows[...] = jnp.where(lax.broadcasted_iota(jnp.int32, (LANES, t), 0) < BIAS_PARTS, 1.0, 0.0).astype(BF16)
    row2 = lax.broadcasted_iota(jnp.int32, (2 * t, LANES), 0)
    lane2 = lax.broadcasted_iota(jnp.int32, (2 * t, LANES), 1)
    qms = []
    for g in range(ATT_GROUPS):
        q2 = q_ref[:, g * LANES:(g + 1) * LANES].astype(F32)
        qm = jnp.concatenate([jnp.where(lane < HEAD_DIM, q2, 0.0),
                              jnp.where(lane >= HEAD_DIM, q2, 0.0)], axis=0).astype(BF16)
        head = (hg * ATT_GROUPS + g) * 2
        rest = jnp.where(row2 < t, bias_ref[head], bias_ref[head + 1])
        bias_cols = jnp.zeros((2 * t, LANES), F32)
        for part in range(BIAS_PARTS):
            piece = rest.astype(BF16).astype(F32)
            bias_cols = jnp.where(lane2 == part, piece, bias_cols)
            rest = rest - piece
        qms.append(jnp.concatenate([qm, bias_cols.astype(BF16)], axis=1))
    oacc[...] = jnp.zeros_like(oacc)
    acc[...] = jnp.zeros_like(acc)

    def block(j, masked):
        valid = kpos < qpos
        zs, sps = [], []
        for g in range(ATT_GROUPS):
            rows = pl.ds(g * LANES, LANES)
            kt_aug = jnp.concatenate([kt_ref[0, j, rows, :], ones_rows[...]], axis=0)
            z = jnp.dot(qms[g], kt_aug, preferred_element_type=F32)
            sp = _softplus(z)
            if masked:
                sp = jnp.where(valid, sp, 0.0)
            zs.append(z)
            sps.append(sp.astype(BF16))
        csums = jnp.dot(jnp.concatenate(sps, axis=0), neg_upper, preferred_element_type=F32)
        for g in range(ATT_GROUPS):
            rows = pl.ds(g * LANES, LANES)
            csum = csums[g * 2 * t:(g + 1) * 2 * t]
            a = acc[g]
            w = jnp.exp(zs[g] + csum + jnp.concatenate([a] * (t // LANES), axis=1))
            if masked:
                w = jnp.where(valid, w, 0.0)
            oacc[g] += lax.dot_general(w.astype(BF16), vt_ref[0, j, rows, :], NT_DIMS,
                                       preferred_element_type=F32)
            acc[g] = a + csum[:, 0:1]

    block(i, True)

    def body(n, carry):
        j = i - 1 - ATT_UNROLL * n
        for u in range(ATT_UNROLL):
            block(j - u, False)
        return carry

    lax.fori_loop(0, i // ATT_UNROLL, body, 0)

    def tail(n, carry):
        block(i % ATT_UNROLL - 1 - n, False)
        return carry

    lax.fori_loop(0, i % ATT_UNROLL, tail, 0)

    for g in range(ATT_GROUPS):
        o = oacc[g]
        o_ref[:, g * LANES:(g + 1) * LANES] = jnp.where(lane < HEAD_DIM, o[:t], o[t:]).astype(BF16)


def _attn_prompt(bias, qb, ktb, vtb, n_seq, seq_len):
    t = ATT_T
    nq = seq_len // t
    width = ATT_GROUPS * LANES
    ngrp = D_MODEL // width
    kv_spec = pl.BlockSpec((1, nq, width, t), lambda b, hg, i, *_: (b, 0, hg, 0))
    q_spec = pl.BlockSpec((t, width), lambda b, hg, i, *_: (b * nq + i, hg))
    return pl.pallas_call(
        _attn_prompt_kernel,
        grid_spec=pltpu.PrefetchScalarGridSpec(
            num_scalar_prefetch=1,
            grid=(n_seq, ngrp, nq),
            in_specs=[q_spec, kv_spec, kv_spec],
            out_specs=q_spec,
            scratch_shapes=[pltpu.VMEM((ATT_GROUPS, 2 * t, LANES), F32),
                            pltpu.VMEM((ATT_GROUPS, 2 * t, LANES), F32),
                            pltpu.VMEM((LANES, t), BF16)],
        ),
        out_shape=jax.ShapeDtypeStruct((n_seq * seq_len, D_MODEL), BF16),
        compiler_params=pltpu.CompilerParams(
            dimension_semantics=("arbitrary", "arbitrary", "arbitrary"), vmem_limit_bytes=VMEM_LIMIT),
        name="attn_prompt",
    )(bias, qb, ktb, vtb)


def _attn_sample_kernel(pt_ref, q_ref, kn_ref, vn_ref, bias_ref, *refs):
    del pt_ref
    p = SAMPLE_PAGES
    k_refs = refs[:p]
    v_refs = refs[p:2 * p]
    o_ref = refs[2 * p]
    qbd, oacc, acc, kcat, vcat = refs[2 * p + 1:]
    c = pl.program_id(1)
    nrow = q_ref.shape[1] * N_HEADS
    n_new = q_ref.shape[1]
    row = lax.broadcasted_iota(jnp.int32, (N_HEADS, D_MODEL), 0)
    col_head = lax.broadcasted_iota(jnp.int32, (N_HEADS, D_MODEL), 1) // HEAD_DIM
    head_mask = row == col_head
    upper = _strict_upper(PAGE)
    bias = bias_ref[...]

    def fold(z, w_to_out, valid=None):
        z = z + bias
        l1m = _neg_softplus(z)
        if valid is not None:
            l1m = jnp.where(valid, l1m, 0.0)
        suffix = jnp.dot(l1m.astype(BF16), upper, preferred_element_type=F32)
        a = acc[...]
        w = jnp.exp((z + l1m) + suffix + a)
        if valid is not None:
            w = jnp.where(valid, w, 0.0)
        oacc[...] += w_to_out(w.astype(BF16))
        acc[...] = a + jnp.sum(l1m, axis=-1, keepdims=True)

    @pl.when(c == 0)
    def _():
        q4 = q_ref[0]
        qbd[...] = jnp.concatenate(
            [jnp.where(head_mask, jnp.broadcast_to(q4[t:t + 1, :], (N_HEADS, D_MODEL)), 0.0)
             for t in range(n_new)], axis=0).astype(BF16)
        oacc[...] = jnp.zeros_like(oacc)
        acc[...] = jnp.zeros_like(acc)
        pad = jnp.zeros((PAGE - kn_ref.shape[1], D_MODEL), F32)
        kn = jnp.concatenate([kn_ref[0], pad], axis=0).astype(BF16)
        vn = jnp.concatenate([vn_ref[0], pad], axis=0).astype(BF16)
        z = lax.dot_general(qbd[...], kn, NT_DIMS, preferred_element_type=F32)
        qtok = lax.broadcasted_iota(jnp.int32, (nrow, PAGE), 0) // N_HEADS
        ktok = lax.broadcasted_iota(jnp.int32, (nrow, PAGE), 1)
        fold(z, lambda w: jnp.dot(w, vn, preferred_element_type=F32), valid=ktok < qtok)

    for r in range(p):
        kcat[:, r * PAGE:(r + 1) * PAGE] = k_refs[r][0].astype(BF16)
        vcat[:, r * PAGE:(r + 1) * PAGE] = v_refs[r][0].astype(BF16)
    blk = SAMPLE_BLOCK
    nblk = p * PAGE // blk
    z = jnp.dot(qbd[...], kcat[...], preferred_element_type=F32) + jnp.concatenate([bias] * p, axis=1)
    sp = _softplus(z)
    sp_rows = jnp.concatenate([sp[:, b * blk:(b + 1) * blk] for b in range(nblk)], axis=0).astype(BF16)
    csum = jnp.dot(sp_rows, _neg_upper_incl(blk), preferred_element_type=F32)
    a = acc[...]
    ws = [None] * nblk
    for b in reversed(range(nblk)):
        cb = csum[b * nrow:(b + 1) * nrow]
        ws[b] = jnp.exp(z[:, b * blk:(b + 1) * blk] + cb + jnp.concatenate([a] * (blk // LANES), axis=1))
        a = a + cb[:, 0:1]
    acc[...] = a
    w = jnp.concatenate(ws, axis=1).astype(BF16)
    oacc[...] += lax.dot_general(w, vcat[...], NT_DIMS, preferred_element_type=F32)

    @pl.when(c == pl.num_programs(1) - 1)
    def _():
        o = oacc[...]
        for t in range(n_new):
            blk = jnp.where(head_mask, o[t * N_HEADS:(t + 1) * N_HEADS], 0.0)
            o_ref[0, pl.ds(t, 1), :] = jnp.sum(blk, axis=0, keepdims=True)


def _attn_sample(pt_t, q, kn, vn, bias_rows, kt_pages, vt_pages):
    n_pages, n_seq = pt_t.shape
    n_new = q.shape[1]
    p = SAMPLE_PAGES
    nch = n_pages // p
    nrow = n_new * N_HEADS

    def page_spec(r):
        return pl.BlockSpec((1, D_MODEL, PAGE), lambda n, c, pt: (pt[(nch - 1 - c) * p + r, n], 0, 0))

    per_seq = lambda rows: pl.BlockSpec((1, rows, D_MODEL), lambda n, c, pt: (n, 0, 0))
    return pl.pallas_call(
        _attn_sample_kernel,
        grid_spec=pltpu.PrefetchScalarGridSpec(
            num_scalar_prefetch=1,
            grid=(n_seq, nch),
            in_specs=[per_seq(n_new), per_seq(kn.shape[1]), per_seq(vn.shape[1]),
                      pl.BlockSpec((nrow, LANES), lambda n, c, pt: (0, 0))]
                     + [page_spec(r) for r in range(p)] + [page_spec(r) for r in range(p)],
            out_specs=per_seq(n_new),
            scratch_shapes=[pltpu.VMEM((nrow, D_MODEL), BF16), pltpu.VMEM((nrow, D_MODEL), F32),
                            pltpu.VMEM((nrow, LANES), F32),
                            pltpu.VMEM((D_MODEL, p * PAGE), BF16), pltpu.VMEM((D_MODEL, p * PAGE), BF16)],
        ),
        out_shape=jax.ShapeDtypeStruct((n_seq, n_new, D_MODEL), F32),
        compiler_params=pltpu.CompilerParams(
            dimension_semantics=("arbitrary", "arbitrary"), vmem_limit_bytes=VMEM_LIMIT),
        name="attn_sample",
    )(pt_t, q, kn, vn, bias_rows, *([kt_pages] * p), *([vt_pages] * p))


def _conv_sample_kernel(state_ref, glu_ref, w_ref, b_ref, o_ref):
    n_state = state_ref.shape[0]
    n_new = glu_ref.shape[0]
    for t in range(n_new):
        a = jnp.broadcast_to(b_ref[...], o_ref.shape[1:])
        for j in range(CONV_K):
            i = t + j
            src = state_ref[i] if i < n_state else glu_ref[i - n_state]
            a = a + w_ref[pl.ds(j, 1), :] * src
        o_ref[t] = a


def _conv_sample(state, glu, w_dw, b_dw):
    n_state, n, ch = state.shape
    n_new = glu.shape[0]
    cw = LANES
    return pl.pallas_call(
        _conv_sample_kernel,
        grid=(ch // cw,),
        in_specs=[pl.BlockSpec((n_state, n, cw), lambda c: (0, 0, c)),
                  pl.BlockSpec((n_new, n, cw), lambda c: (0, 0, c)),
                  pl.BlockSpec((CONV_K, cw), lambda c: (0, c)),
                  pl.BlockSpec((1, cw), lambda c: (0, c))],
        out_specs=pl.BlockSpec((n_new, n, cw), lambda c: (0, 0, c)),
        out_shape=jax.ShapeDtypeStruct((n_new, n, ch), F32),
        compiler_params=pltpu.CompilerParams(dimension_semantics=("arbitrary",), vmem_limit_bytes=VMEM_LIMIT),
        name="conv_sample",
    )(state, glu, w_dw, b_dw)


def _finish_core(x, gate, att, sga, conv, sgc, sma, smc, lng, lnb, wa, wc, wo, gf):
    mu = jnp.mean(conv, axis=-1, keepdims=True)
    d = conv - mu
    ln = d * lax.rsqrt(jnp.mean(d * d, axis=-1, keepdims=True) + EPS) * lng + lnb
    cv = (_silu(ln) * sgc.astype(F32)).astype(BF16)
    av = (att.astype(F32) * sga.astype(F32)).astype(BF16)
    y_a = jnp.dot(av, wa, preferred_element_type=F32)
    y_c = jnp.dot(cv, wc, preferred_element_type=F32)
    merged = (sma.astype(F32) * y_a + smc.astype(F32) * y_c).astype(BF16)
    y = x + gate * jnp.dot(merged, wo, preferred_element_type=F32)
    return y * lax.rsqrt(jnp.mean(y * y, axis=-1, keepdims=True) + EPS) * gf


def _finish_prompt_kernel(x_ref, gate_ref, att_ref, sga_ref, glu_ref, halo_ref, sgc_ref, sma_ref, smc_ref,
                          wdw_ref, bdw_ref, lng_ref, lnb_ref, wa_ref, wc_ref, wo_ref, gf_ref, o_ref,
                          buf, shifted, conv_scr, *, blocks_per_seq):
    tm = x_ref.shape[0]
    first = (pl.program_id(0) % blocks_per_seq) == 0
    halo = halo_ref[...]
    buf[:HALO] = jnp.where(first, jnp.zeros_like(halo), halo)
    buf[HALO:] = glu_ref[...]
    base = HALO - (CONV_K - 1)
    srows = tm + HALO - SUBLANES
    for c in range(D_MODEL // LANES):
        cols = pl.ds(c * LANES, LANES)
        for sub in range(1, SUBLANES):
            shifted[sub - 1, :, cols] = buf[pl.ds(sub, srows), cols]
        acc = jnp.broadcast_to(bdw_ref[:, cols], (tm, LANES))
        for j in range(CONV_K):
            sub, start = (base + j) % SUBLANES, (base + j) // SUBLANES * SUBLANES
            src = buf[pl.ds(start, tm), cols] if sub == 0 else shifted[sub - 1, pl.ds(start, tm), cols]
            acc = acc + wdw_ref[pl.ds(j, 1), cols] * src
        conv_scr[:, cols] = acc
    conv = conv_scr[...]
    o_ref[...] = _finish_core(x_ref[...], gate_ref[0], att_ref[...], sga_ref[...], conv, sgc_ref[...],
                              sma_ref[...], smc_ref[...], lng_ref[...], lnb_ref[...], wa_ref[...],
                              wc_ref[...], wo_ref[...], gf_ref[...])


def _finish_sample_kernel(x_ref, gate_ref, att_ref, sga_ref, conv_ref, sgc_ref, sma_ref, smc_ref,
                          lng_ref, lnb_ref, wa_ref, wc_ref, wo_ref, gf_ref, o_ref):
    o_ref[...] = _finish_core(x_ref[...], gate_ref[0], att_ref[...], sga_ref[...], conv_ref[...], sgc_ref[...],
                              sma_ref[...], smc_ref[...], lng_ref[...], lnb_ref[...], wa_ref[...],
                              wc_ref[...], wo_ref[...], gf_ref[...])


def _const_spec(shape):
    return pl.BlockSpec(shape, lambda i: (0,) * len(shape))


def _finish_prompt(x, gate, att, sga, glu, sgc, sma, smc, wdw, bdw, lng, lnb, wa, wc, wo, gf, seq_len):
    t = x.shape[0]
    tm = FIN_TM
    bps = seq_len // tm
    tok = pl.BlockSpec((tm, D_MODEL), lambda i: (i, 0))
    vec = _const_spec((1, D_MODEL))
    mat = _const_spec((D_MODEL, D_MODEL))
    halo_spec = pl.BlockSpec((HALO, D_MODEL), lambda i: (jnp.maximum(i * (tm // HALO) - 1, 0), 0))
    return pl.pallas_call(
        functools.partial(_finish_prompt_kernel, blocks_per_seq=bps),
        grid=(t // tm,),
        in_specs=[tok, pl.BlockSpec((1, 1, D_MODEL), lambda i: (i // bps, 0, 0)), tok, tok, tok, halo_spec,
                  tok, tok, tok, _const_spec((CONV_K, D_MODEL)), vec, vec, vec, mat, mat, mat, vec],
        out_specs=tok,
        out_shape=jax.ShapeDtypeStruct((t, D_MODEL), F32),
        scratch_shapes=[pltpu.VMEM((tm + HALO, D_MODEL), F32),
                        pltpu.VMEM((SUBLANES - 1, tm + HALO - SUBLANES, D_MODEL), F32),
                        pltpu.VMEM((tm, D_MODEL), F32)],
        compiler_params=pltpu.CompilerParams(dimension_semantics=("arbitrary",), vmem_limit_bytes=VMEM_LIMIT),
        name="finish_prompt",
    )(x, gate, att, sga, glu, glu, sgc, sma, smc, wdw, bdw, lng, lnb, wa, wc, wo, gf)


def _finish_sample(x, gate, att, sga, conv, sgc, sma, smc, lng, lnb, wa, wc, wo, gf):
    t = x.shape[0]
    tok = _const_spec((t, D_MODEL))
    vec = _const_spec((1, D_MODEL))
    mat = _const_spec((D_MODEL, D_MODEL))
    return pl.pallas_call(
        _finish_sample_kernel,
        grid=(1,),
        in_specs=[tok, _const_spec((1, t, D_MODEL)), tok, tok, tok, tok, tok, tok, vec, vec, mat, mat, mat, vec],
        out_specs=tok,
        out_shape=jax.ShapeDtypeStruct((t, D_MODEL), F32),
        compiler_params=pltpu.CompilerParams(dimension_semantics=("arbitrary",), vmem_limit_bytes=VMEM_LIMIT),
        name="finish_sample",
    )(x, gate, att, sga, conv, sgc, sma, smc, lng, lnb, wa, wc, wo, gf)


def kernel(x_prompt, x_sample, cache_k, cache_v, state_conv, page_table, c_prompt, c_sample, w_ada, b_ada, g_norm, w_in, b_sb, w_dw, b_dw, ln_g, ln_b, w_a, w_c, w_out, g_final):
    assert w_in.shape[0] == 1, "one layer"
    n_p, s_p, d = x_prompt.shape
    n_s, s_s, _ = x_sample.shape
    n_state = state_conv.shape[2]
    half = d // 2

    w = w_in[0]
    grp = [w[:, g * d:(g + 1) * d] for g in range(9)]
    w9 = jnp.stack([
        grp[0], grp[1].T, grp[2].T, grp[3],
        jnp.concatenate([grp[4][:, :half], grp[5][:, :half]], axis=1),
        jnp.concatenate([grp[4][:, half:], grp[5][:, half:]], axis=1),
        grp[6], grp[7], grp[8]]).astype(BF16)
    wa, wc, wo = w_a[0].astype(BF16), w_c[0].astype(BF16), w_out[0].astype(BF16)
    g1, gf = g_norm[0][None, :], g_final[None, :]
    lng, lnb, bdw, wdw = ln_g[0][None, :], ln_b[0][None, :], b_dw[0][None, :], w_dw[0]
    bias = b_sb[0]

    mod = _mod(jnp.concatenate([c_prompt, c_sample], axis=0), w_ada[0].astype(BF16), b_ada[0][None, :])
    mod_p = mod[:n_p].reshape(n_p, 3, d)
    mod_s = mod[n_p:].reshape(n_s, 3, d)
    mod_s_rows = jnp.broadcast_to(mod_s[None], (s_s, n_s, 3, d)).reshape(s_s * n_s, 3, d)

    xp = x_prompt.reshape(n_p * s_p, d)
    shift_p, scale_p, gate_p = (mod_p[:, c][:, None, :] for c in range(3))
    qb, kt, ktb, vt, vtb, sga, glu, sgc, sma, smc = _proj(xp, shift_p, scale_p, g1, w9, n_p, s_p, False)
    att = _attn_prompt(bias, qb, ktb, vtb, n_p, s_p)
    y_p = _finish_prompt(xp, gate_p, att, sga, glu, sgc, sma, smc, wdw, bdw, lng, lnb, wa, wc, wo, gf, s_p)
    y_prompt = y_p.reshape(n_p, s_p, d)
    to_heads_t = lambda a: a.reshape(a.shape[0], N_HEADS, HEAD_DIM, a.shape[2]).transpose(0, 3, 1, 2)[None]
    k_prompt, v_prompt = to_heads_t(kt), to_heads_t(vt)
    conv_prompt = glu.reshape(n_p, s_p, d)[:, s_p - (CONV_K - 1):][None]

    xs = x_sample.transpose(1, 0, 2).reshape(s_s * n_s, d)
    shift_s, scale_s, gate_s = (mod_s_rows[:, c][None] for c in range(3))
    qb_s, kt_s, _, vt_s, _, sga_s, glu_s, sgc_s, sma_s, smc_s = _proj(
        xs, shift_s, scale_s, g1, w9, 1, s_s * n_s, True)
    tok_major = lambda a_t: a_t[0].T.reshape(s_s, n_s, d)
    k_new, v_new = tok_major(kt_s), tok_major(vt_s)
    seq_major = lambda a: a.transpose(1, 0, 2)
    pad_rows = lambda a: jnp.pad(a, ((0, 0), (0, SUBLANES - s_s), (0, 0)))
    q_seq = seq_major(qb_s.astype(F32).reshape(s_s, n_s, d))
    pages_t = lambda c: c[0].transpose(0, 2, 3, 1).reshape(c.shape[1], d, PAGE)
    bias_rows = jnp.broadcast_to(jnp.tile(bias, s_s)[:, None], (s_s * N_HEADS, LANES))
    att_s = _attn_sample(page_table.T, q_seq, pad_rows(seq_major(k_new)), pad_rows(seq_major(v_new)),
                         bias_rows, pages_t(cache_k), pages_t(cache_v))
    att_s = seq_major(att_s).reshape(s_s * n_s, d).astype(BF16)
    state_t = state_conv[0].transpose(1, 0, 2)
    glu_s3 = glu_s.reshape(s_s, n_s, d)
    conv_s = _conv_sample(state_t, glu_s3, wdw, bdw).reshape(s_s * n_s, d)
    y_s = _finish_sample(xs, gate_s, att_s, sga_s, conv_s, sgc_s, sma_s, smc_s, lng, lnb, wa, wc, wo, gf)
    y_sample = seq_major(y_s.reshape(s_s, n_s, d))
    heads = lambda a: seq_major(a).reshape(n_s, s_s, N_HEADS, HEAD_DIM)[None]
    k_sample, v_sample = heads(k_new), heads(v_new)
    conv_sample = seq_major(jnp.concatenate([state_t, glu_s3], axis=0)[s_s:])[None]
    assert n_state == CONV_K - 1
    return (y_prompt, y_sample, k_prompt, v_prompt, conv_prompt, k_sample, v_sample, conv_sample)
```
